```python
import math
import jax, jax.numpy as jnp
from jax import lax
import numpy as np

D_MODEL = 1024
BATCH = 4
SEQ = 8192
DEPTH = 4

CHUNK = 64
N_PREV = 8
BAND = (N_PREV + 1) * CHUNK
N_HEADS = 16
HEAD_DIM = D_MODEL // N_HEADS
E_MIX = N_HEADS * HEAD_DIM
REL_CLIP = 128
N_REL = 2 * REL_CLIP + 1
CONV_W = 3
N_MEM = 256
MEM_HEADS = 4
MEM_HEAD_DIM = 128
E_MEM = MEM_HEADS * MEM_HEAD_DIM
E_BRANCH = E_MIX + E_MEM
N_IN = 3 * E_MIX + E_MEM + E_BRANCH
N_MIXERS = 2
N_ATTN_LAYERS = (DEPTH + 1) // 2
N_CONV_LAYERS = DEPTH // 2
DN_ALPHA = (2.0 * DEPTH) ** 0.25
DN_BETA = (8.0 * DEPTH) ** -0.25
LN_EPS = 1e-5

kernel_name = "hybrid_chunk_attn_shortconv_mem_deepnorm"


def layer_norm(x, g, b):
    xf = x.astype(jnp.float32)
    mu = jnp.mean(xf, axis=-1, keepdims=True)
    var = jnp.mean(jnp.square(xf - mu), axis=-1, keepdims=True)
    y = (xf - mu) * lax.rsqrt(var + LN_EPS) * g.astype(jnp.float32) + b.astype(jnp.float32)
    return y.astype(x.dtype)


def rel_bias_band(table):
    i = jnp.arange(CHUNK)[:, None]
    m = jnp.arange(BAND)[None, :]
    rel = N_PREV * CHUNK + i - m
    idx = jnp.clip(rel, -REL_CLIP, REL_CLIP) + REL_CLIP
    return table[:, idx]


def chunked_attention(q, k, v, bias_table):
    b, s, h, dh = q.shape
    n_chunks = s // CHUNK
    pad = ((0, 0), (N_PREV * CHUNK, 0), (0, 0), (0, 0))
    k_pad = jnp.pad(k, pad)
    v_pad = jnp.pad(v, pad)
    bias = rel_bias_band(bias_table).astype(jnp.float32)
    scale = 1.0 / math.sqrt(dh)
    q_blocks = jnp.moveaxis(q.reshape(b, n_chunks, CHUNK, h, dh), 1, 0)
    neg = jnp.finfo(jnp.float32).min

    def one_chunk(args):
        q_blk, c = args
        k_band = lax.dynamic_slice_in_dim(k_pad, c * CHUNK, BAND, axis=1)
        v_band = lax.dynamic_slice_in_dim(v_pad, c * CHUNK, BAND, axis=1)
        sc = jnp.einsum('bqhd,bkhd->bhqk', q_blk, k_band).astype(jnp.float32) * scale + bias[None]
        key_pos = (c - N_PREV) * CHUNK + jnp.arange(BAND)
        sc = jnp.where((key_pos >= 0)[None, None, None, :], sc, neg)
        p = jax.nn.softmax(sc, axis=-1).astype(v_band.dtype)
        return jnp.einsum('bhqk,bkhd->bqhd', p, v_band)

    out = lax.map(one_chunk, (q_blocks, jnp.arange(n_chunks)))
    return jnp.moveaxis(out, 0, 1).reshape(b, s, h * dh)


def causal_dwconv(u, w):
    c = u.shape[-1]
    return lax.conv_general_dilated(
        u, w[:, None, :].astype(u.dtype), window_strides=(1,),
        padding=[(CONV_W - 1, 0)], dimension_numbers=('NWC', 'WIO', 'NWC'),
        feature_group_count=c)


def short_gated_conv(bg, cg, u, w):
    return bg * causal_dwconv(cg * u, w)


def memory_attention(q_mem, kv_mem):
    b, s, _ = q_mem.shape
    q = q_mem.reshape(b, s, MEM_HEADS, MEM_HEAD_DIM)
    k, v = jnp.split(kv_mem, 2, axis=-1)
    k = k.reshape(b, -1, MEM_HEADS, MEM_HEAD_DIM)
    v = v.reshape(b, -1, MEM_HEADS, MEM_HEAD_DIM)
    sc = jnp.einsum('bshd,bmhd->bhsm', q, k).astype(jnp.float32) / math.sqrt(MEM_HEAD_DIM)
    p = jax.nn.softmax(sc, axis=-1).astype(v.dtype)
    return jnp.einsum('bhsm,bmhd->bshd', p, v).reshape(b, s, E_MEM)


def setup_inputs(seed: int = 0) -> dict:
    key = jax.random.key(seed)
    ks = jax.random.split(key, 10)
    x = jax.random.normal(ks[0], (BATCH, SEQ, D_MODEL), jnp.float32)
    mem = jax.random.normal(ks[1], (BATCH, N_MEM, D_MODEL), jnp.float32)
    w_in = jax.random.normal(ks[2], (DEPTH, D_MODEL, N_IN), jnp.float32) * D_MODEL ** -0.5
    w_mem_kv = jax.random.normal(ks[3], (DEPTH, D_MODEL, 2 * E_MEM), jnp.float32) * D_MODEL ** -0.5
    w_out = jax.random.normal(ks[4], (DEPTH, E_BRANCH, D_MODEL), jnp.float32) * (E_BRANCH ** -0.5 * DN_BETA)
    rel_bias = jax.random.normal(ks[5], (N_ATTN_LAYERS, N_HEADS, N_REL), jnp.float32) * 0.5
    conv_w = jax.random.normal(ks[6], (N_CONV_LAYERS, CONV_W, E_MIX), jnp.float32) * CONV_W ** -0.5
    ln_g = 1.0 + 0.05 * jax.random.normal(ks[7], (DEPTH, D_MODEL), jnp.float32)
    ln_b = 0.02 * jax.random.normal(ks[8], (DEPTH, D_MODEL), jnp.float32)
    return {"x": x, "mem": mem, "w_in": w_in, "w_mem_kv": w_mem_kv, "w_out": w_out,
            "rel_bias": rel_bias, "conv_w": conv_w, "ln_g": ln_g, "ln_b": ln_b}


def reference(x, mem, w_in, w_mem_kv, w_out, rel_bias, conv_w, ln_g, ln_b):
    b, s, _ = x.shape
    for layer in range(DEPTH):
        h = jnp.einsum('bsd,de->bse', x, w_in[layer])
        mix_in = h[..., :3 * E_MIX]
        q_mem = h[..., 3 * E_MIX:3 * E_MIX + E_MEM]
        z = h[..., 3 * E_MIX + E_MEM:]
        p0, p1, p2 = jnp.split(mix_in, 3, axis=-1)
        if layer % N_MIXERS == 0:
            q = p0.reshape(b, s, N_HEADS, HEAD_DIM)
            k = p1.reshape(b, s, N_HEADS, HEAD_DIM)
            v = p2.reshape(b, s, N_HEADS, HEAD_DIM)
            mix_out = chunked_attention(q, k, v, rel_bias[layer // N_MIXERS])
        else:
            mix_out = short_gated_conv(p0, p1, p2, conv_w[layer // N_MIXERS])
        kv_mem = jnp.einsum('bmd,de->bme', mem, w_mem_kv[layer])
        mem_out = memory_attention(q_mem, kv_mem)
        y = jnp.concatenate([mix_out, mem_out], axis=-1) * jax.nn.silu(z)
        out = jnp.einsum('bse,ed->bsd', y, w_out[layer])
        x = layer_norm(DN_ALPHA * x + out, ln_g[layer], ln_b[layer])
    return x
```

```python
import functools
import math

import jax
import jax.numpy as jnp
from jax import lax
from jax.experimental import pallas as pl
from jax.experimental.pallas import tpu as pltpu

D_MODEL = 1024
DEPTH = 4
CHUNK = 64
N_PREV = 8
N_HEADS = 16
HEAD_DIM = 64
E_MIX = N_HEADS * HEAD_DIM
REL_CLIP = 128
N_REL = 2 * REL_CLIP + 1
CONV_W = 3
N_MEM = 256
MEM_HEADS = 4
MEM_HEAD_DIM = 128
E_MEM = MEM_HEADS * MEM_HEAD_DIM
E_BRANCH = E_MIX + E_MEM
N_IN = 3 * E_MIX + E_MEM + E_BRANCH
DN_ALPHA = (2.0 * DEPTH) ** 0.25
LN_EPS = 1e-5

LANES = 128
TILE = 256
HIST = N_PREV * CHUNK
WIN = HIST + TILE
N_PAIRS = E_MIX // LANES
HEADS_PER_PAIR = LANES // HEAD_DIM
REL_PAD = 384
BASE_W = 1024
NEG = -1e30
VMEM_LIMIT = 56 * 1024 * 1024

_Q0, _K0, _V0 = 0, E_MIX, 2 * E_MIX
_QM0 = 3 * E_MIX
_Z0 = 3 * E_MIX + E_MEM

bf16 = jnp.bfloat16
f32 = jnp.float32


def _dot(a, b):
    return jnp.dot(a, b, preferred_element_type=f32)


def _dot_nt(a, b):
    return lax.dot_general(a, b, (((1,), (1,)), ((), ())), preferred_element_type=f32)


def _softmax_pv(s, v):
    m = jnp.max(s, axis=-1, keepdims=True)
    p = jnp.exp(s - m)
    l = jnp.sum(p, axis=-1, keepdims=True)
    return _dot(p.astype(bf16), v) / l


def _kvmem_kernel(mem_ref, w_ref, o_ref):
    o_ref[...] = _dot(mem_ref[...].astype(bf16), w_ref[...]).astype(bf16)


def _kv_mem_all(mem2d, w_mem_kv_bf):
    rows = mem2d.shape[0]
    return pl.pallas_call(
        _kvmem_kernel,
        grid=(DEPTH,),
        in_specs=[
            pl.BlockSpec((rows, D_MODEL), lambda l: (0, 0)),
            pl.BlockSpec((None, D_MODEL, 2 * E_MEM), lambda l: (l, 0, 0)),
        ],
        out_specs=pl.BlockSpec((None, rows, 2 * E_MEM), lambda l: (l, 0, 0)),
        out_shape=jax.ShapeDtypeStruct((DEPTH, rows, 2 * E_MEM), bf16),
        compiler_params=pltpu.CompilerParams(
            dimension_semantics=("arbitrary",), vmem_limit_bytes=VMEM_LIMIT),
        name="kv_mem_proj",
    )(mem2d, w_mem_kv_bf)


def _bias_kernel(table_ref, o_ref):
    t = table_ref[...]
    c = lax.broadcasted_iota(jnp.int32, (REL_PAD, BASE_W), 1)
    j = lax.broadcasted_iota(jnp.int32, (REL_PAD, BASE_W), 0)
    idx = jnp.clip(HIST + TILE - c, -REL_CLIP, REL_CLIP) + REL_CLIP
    onehot = jnp.where(j == idx, 1.0, 0.0).astype(bf16)
    hi = t.astype(bf16)
    r1 = t - hi.astype(f32)
    mid = r1.astype(bf16)
    lo = (r1 - mid.astype(f32)).astype(bf16)
    base = _dot(hi, onehot) + _dot(mid, onehot) + _dot(lo, onehot)

    kchunk = lax.broadcasted_iota(jnp.int32, (N_HEADS, WIN), 1) // CHUNK

    def row(r, carry):
        rolled = pltpu.roll(base, r, 1)[:, TILE:]
        d = kchunk - r // CHUNK
        val = jnp.where((d >= 0) & (d <= N_PREV), rolled, NEG)
        o_ref[:, pl.ds(r, 1), :] = val[:, None, :]
        return carry

    lax.fori_loop(0, TILE, row, 0)


def _tile_bias(table_pad):
    return pl.pallas_call(
        _bias_kernel,
        out_shape=jax.ShapeDtypeStruct((N_HEADS, TILE, WIN), f32),
        compiler_params=pltpu.CompilerParams(vmem_limit_bytes=VMEM_LIMIT),
        name="rel_bias_tile",
    )(table_pad)


def _layer_tail(xf, xb, mix, w_in_ref, w_out_ref, kvm_ref, g_ref, b_ref, o_ref):
    qm = _dot(xb, w_in_ref[:, _QM0:_QM0 + E_MEM]) * (1.0 / math.sqrt(MEM_HEAD_DIM))
    mem_outs = []
    for h in range(MEM_HEADS):
        lo = h * MEM_HEAD_DIM
        qh = qm[:, lo:lo + MEM_HEAD_DIM].astype(bf16)
        kh = kvm_ref[:, lo:lo + MEM_HEAD_DIM]
        vh = kvm_ref[:, E_MEM + lo:E_MEM + lo + MEM_HEAD_DIM]
        mem_outs.append(_softmax_pv(_dot_nt(qh, kh), vh))
    z = _dot(xb, w_in_ref[:, _Z0:_Z0 + E_BRANCH])
    gate = z * jax.nn.sigmoid(z)
    y = jnp.concatenate([mix] + mem_outs, axis=-1) * gate
    out = _dot(y.astype(bf16), w_out_ref[...])
    u = DN_ALPHA * xf + out
    mu = jnp.mean(u, axis=-1, keepdims=True)
    uc = u - mu
    var = jnp.mean(uc * uc, axis=-1, keepdims=True)
    o_ref[...] = uc * lax.rsqrt(var + LN_EPS) * g_ref[...] + b_ref[...]


def _attn_layer_kernel(x_ref, w_in_ref, w_out_ref, kvm_ref, bias_ref, g_ref, b_ref, o_ref,
                       q_scr, k_scr, v_scr, mix_scr):
    si = pl.program_id(1)
    xf = x_ref[...]
    xb = xf.astype(bf16)

    @pl.when(si == 0)
    def _():
        k_scr[:, 0:HIST, :] = jnp.zeros((N_PAIRS, HIST, LANES), bf16)
        v_scr[:, 0:HIST, :] = jnp.zeros((N_PAIRS, HIST, LANES), bf16)

    @pl.when(si > 0)
    def _():
        for c in range(HIST // TILE):
            k_scr[:, c * TILE:(c + 1) * TILE, :] = k_scr[:, (c + 1) * TILE:(c + 2) * TILE, :]
            v_scr[:, c * TILE:(c + 1) * TILE, :] = v_scr[:, (c + 1) * TILE:(c + 2) * TILE, :]

    q = _dot(xb, w_in_ref[:, _Q0:_Q0 + E_MIX]) * (1.0 / math.sqrt(HEAD_DIM))
    for hp in range(N_PAIRS):
        q_scr[hp] = q[:, hp * LANES:(hp + 1) * LANES].astype(bf16)
    k = _dot(xb, w_in_ref[:, _K0:_K0 + E_MIX])
    for hp in range(N_PAIRS):
        k_scr[hp, HIST:WIN, :] = k[:, hp * LANES:(hp + 1) * LANES].astype(bf16)
    v = _dot(xb, w_in_ref[:, _V0:_V0 + E_MIX])
    for hp in range(N_PAIRS):
        v_scr[hp, HIST:WIN, :] = v[:, hp * LANES:(hp + 1) * LANES].astype(bf16)

    kpos = si * TILE - HIST + lax.broadcasted_iota(jnp.int32, (1, WIN), 1)
    kmask = jnp.where(kpos >= 0, 0.0, NEG).astype(f32)

    def pair_body(hp, carry):
        qp = q_scr[hp]
        kp = k_scr[hp]
        vp = v_scr[hp]
        outs = []
        for e in range(HEADS_PER_PAIR):
            lo = e * HEAD_DIM
            s = _dot_nt(qp[:, lo:lo + HEAD_DIM], kp[:, lo:lo + HEAD_DIM])
            s = s + bias_ref[HEADS_PER_PAIR * hp + e] + kmask
            outs.append(_softmax_pv(s, vp[:, lo:lo + HEAD_DIM]))
        mix_scr[hp] = jnp.concatenate(outs, axis=-1)
        return carry

    lax.fori_loop(0, N_PAIRS, pair_body, 0)

    mix = jnp.concatenate([mix_scr[hp] for hp in range(N_PAIRS)], axis=-1)
    _layer_tail(xf, xb, mix, w_in_ref, w_out_ref, kvm_ref, g_ref, b_ref, o_ref)


_CARRY = 8


def _conv_layer_kernel(x_ref, w_in_ref, w_out_ref, kvm_ref, cw_ref, g_ref, b_ref, o_ref, cu_scr):
    si = pl.program_id(1)
    xf = x_ref[...]
    xb = xf.astype(bf16)

    @pl.when(si == 0)
    def _():
        cu_scr[0:_CARRY, :] = jnp.zeros((_CARRY, E_MIX), f32)

    @pl.when(si > 0)
    def _():
        cu_scr[0:_CARRY, :] = cu_scr[TILE:TILE + _CARRY, :]

    p0 = _dot(xb, w_in_ref[:, _Q0:_Q0 + E_MIX])
    p1 = _dot(xb, w_in_ref[:, _K0:_K0 + E_MIX])
    p2 = _dot(xb, w_in_ref[:, _V0:_V0 + E_MIX])
    cu_scr[_CARRY:_CARRY + TILE, :] = p1 * p2
    cw = cw_ref[...]
    conv = cu_scr[_CARRY:_CARRY + TILE, :] * cw[CONV_W - 1:CONV_W, :]
    for t in range(1, CONV_W):
        conv = conv + cu_scr[_CARRY - t:_CARRY - t + TILE, :] * cw[CONV_W - 1 - t:CONV_W - t, :]
    mix = p0 * conv
    _layer_tail(xf, xb, mix, w_in_ref, w_out_ref, kvm_ref, g_ref, b_ref, o_ref)


def _const_spec(shape):
    nd = len(shape)
    return pl.BlockSpec(shape, lambda b, s: (0,) * nd, pipeline_mode=pl.Buffered(1))


def _layer_call(kind, x, w_in_l, w_out_l, kvm_l, extra, g, b):
    batch, seq, _ = x.shape
    n_tiles = seq // TILE
    x_spec = pl.BlockSpec((None, TILE, D_MODEL), lambda b, s: (b, s, 0))
    kvm_spec = pl.BlockSpec((None, N_MEM, 2 * E_MEM), lambda b, s: (b, 0, 0))
    common = [x_spec, _const_spec((D_MODEL, N_IN)), _const_spec((E_BRANCH, D_MODEL)), kvm_spec]
    vec_spec = _const_spec((1, D_MODEL))
    if kind == "attn":
        body = _attn_layer_kernel
        in_specs = common + [_const_spec((N_HEADS, TILE, WIN)), vec_spec, vec_spec]
        scratch = [
            pltpu.VMEM((N_PAIRS, TILE, LANES), bf16),
            pltpu.VMEM((N_PAIRS, WIN, LANES), bf16),
            pltpu.VMEM((N_PAIRS, WIN, LANES), bf16),
            pltpu.VMEM((N_PAIRS, TILE, LANES), f32),
        ]
    else:
        body = _conv_layer_kernel
        in_specs = common + [_const_spec((CONV_W, E_MIX)), vec_spec, vec_spec]
        scratch = [pltpu.VMEM((TILE + 2 * _CARRY, E_MIX), f32)]
    return pl.pallas_call(
        body,
        grid=(batch, n_tiles),
        in_specs=in_specs,
        out_specs=pl.BlockSpec((None, TILE, D_MODEL), lambda b, s: (b, s, 0)),
        out_shape=jax.ShapeDtypeStruct(x.shape, f32),
        scratch_shapes=scratch,
        compiler_params=pltpu.CompilerParams(
            dimension_semantics=("arbitrary", "arbitrary"), vmem_limit_bytes=VMEM_LIMIT),
        name=f"{kind}_layer",
    )(x, w_in_l, w_out_l, kvm_l, extra, g, b)


def kernel(x, mem, w_in, w_mem_kv, w_out, rel_bias, conv_w, ln_g, ln_b):
    batch, seq, d = x.shape
    assert d == D_MODEL and seq % TILE == 0 and mem.shape == (batch, N_MEM, D_MODEL)
    w_in_bf = w_in.astype(bf16)
    w_out_bf = w_out.astype(bf16)
    kvm = _kv_mem_all(mem.reshape(batch * N_MEM, D_MODEL), w_mem_kv.astype(bf16))
    kvm = kvm.reshape(DEPTH, batch, N_MEM, 2 * E_MEM)
    table_pad = jnp.pad(rel_bias, ((0, 0), (0, 0), (0, REL_PAD - N_REL)))
    for layer in range(DEPTH):
        g = ln_g[layer].reshape(1, D_MODEL)
        b = ln_b[layer].reshape(1, D_MODEL)
        if layer % 2 == 0:
            extra = _tile_bias(table_pad[layer // 2])
            kind = "attn"
        else:
            extra = conv_w[layer // 2]
            kind = "conv"
        x = _layer_call(kind, x, w_in_bf[layer], w_out_bf[layer], kvm[layer], extra, g, b)
    return x
```

```python
import math

import jax
import jax.numpy as jnp
from jax import lax
from jax.experimental import pallas as pl
from jax.experimental.pallas import tpu as pltpu

D_MODEL = 1024
DEPTH = 4
CHUNK = 64
N_PREV = 8
N_HEADS = 16
HEAD_DIM = 64
E_MIX = N_HEADS * HEAD_DIM
REL_CLIP = 128
N_REL = 2 * REL_CLIP + 1
CONV_W = 3
N_MEM = 256
MEM_HEADS = 4
MEM_HEAD_DIM = 128
E_MEM = MEM_HEADS * MEM_HEAD_DIM
E_BRANCH = E_MIX + E_MEM
N_IN = 3 * E_MIX + E_MEM + E_BRANCH
DN_ALPHA = (2.0 * DEPTH) ** 0.25
LN_EPS = 1e-5

LANES = 128
TILE = 256
HIST = N_PREV * CHUNK
WIN = HIST + TILE
N_PAIRS = E_MIX // LANES
HEADS_PER_PAIR = LANES // HEAD_DIM
REL_PAD = 384
BASE_W = 1024
NEG = -1e30
LOG2E = math.log2(math.e)
VMEM_LIMIT = 56 * 1024 * 1024

_Q0, _K0, _V0 = 0, E_MIX, 2 * E_MIX
_QM0 = 3 * E_MIX
_Z0 = 3 * E_MIX + E_MEM
_AK0, _AQM0, _AZ0 = 0, E_MIX, E_MIX + E_MEM

bf16 = jnp.bfloat16
f32 = jnp.float32


def _dot(a, b):
    return jnp.dot(a, b, preferred_element_type=f32)


def _dot_nt(a, b):
    return lax.dot_general(a, b, (((1,), (1,)), ((), ())), preferred_element_type=f32)


def _softmax2_pv(s, v):
    m = jnp.max(s, axis=-1, keepdims=True)
    p = jnp.exp2(s - m)
    l = jnp.sum(p, axis=-1, keepdims=True)
    return _dot(p.astype(bf16), v) / l


def _kvmem_kernel(mem_ref, w_ref, o_ref):
    o_ref[...] = _dot(mem_ref[...].astype(bf16), w_ref[...]).astype(bf16)


def _kv_mem_all(mem2d, w_mem_kv_bf):
    rows = mem2d.shape[0]
    return pl.pallas_call(
        _kvmem_kernel,
        grid=(DEPTH,),
        in_specs=[
            pl.BlockSpec((rows, D_MODEL), lambda l: (0, 0)),
            pl.BlockSpec((None, D_MODEL, 2 * E_MEM), lambda l: (l, 0, 0)),
        ],
        out_specs=pl.BlockSpec((None, rows, 2 * E_MEM), lambda l: (l, 0, 0)),
        out_shape=jax.ShapeDtypeStruct((DEPTH, rows, 2 * E_MEM), bf16),
        compiler_params=pltpu.CompilerParams(
            dimension_semantics=("arbitrary",), vmem_limit_bytes=VMEM_LIMIT),
        name="kv_mem_proj",
    )(mem2d, w_mem_kv_bf)


_BASE_OFF = TILE - 1


def _bias_kernel(table_ref, o_ref):
    t = table_ref[...]
    c = lax.broadcasted_iota(jnp.int32, (REL_PAD, BASE_W), 1)
    j = lax.broadcasted_iota(jnp.int32, (REL_PAD, BASE_W), 0)
    idx = jnp.clip(c - _BASE_OFF, -REL_CLIP, REL_CLIP) + REL_CLIP
    onehot = jnp.where(j == idx, 1.0, 0.0).astype(bf16)
    hi = t.astype(bf16)
    r1 = t - hi.astype(f32)
    mid = r1.astype(bf16)
    lo = (r1 - mid.astype(f32)).astype(bf16)
    base = (_dot(hi, onehot) + _dot(mid, onehot) + _dot(lo, onehot)) * LOG2E

    qchunk = lax.broadcasted_iota(jnp.int32, (N_HEADS, TILE), 1) // CHUNK

    def row(p, carry):
        rolled = pltpu.roll(base, p + (BASE_W - HIST - _BASE_OFF), 1)[:, 0:TILE]
        d = p // CHUNK - qchunk
        val = jnp.where((d >= 0) & (d <= N_PREV), rolled, NEG)
        o_ref[:, pl.ds(p, 1), :] = val[:, None, :]
        return carry

    lax.fori_loop(0, WIN, row, 0)


def _tile_bias(table_pad):
    return pl.pallas_call(
        _bias_kernel,
        out_shape=jax.ShapeDtypeStruct((N_HEADS, WIN, TILE), f32),
        compiler_params=pltpu.CompilerParams(vmem_limit_bytes=VMEM_LIMIT),
        name="rel_bias_tile",
    )(table_pad)


def _layer_tail(xf, xb, mix, w_qm, w_z, w_out_ref, kvm_ref, g_ref, b_ref, o_ref):
    qm = _dot(xb, w_qm) * (LOG2E / math.sqrt(MEM_HEAD_DIM))
    mem_outs = []
    for h in range(MEM_HEADS):
        lo = h * MEM_HEAD_DIM
        qh = qm[:, lo:lo + MEM_HEAD_DIM].astype(bf16)
        kh = kvm_ref[:, lo:lo + MEM_HEAD_DIM]
        vh = kvm_ref[:, E_MEM + lo:E_MEM + lo + MEM_HEAD_DIM]
        mem_outs.append(_softmax2_pv(_dot_nt(qh, kh), vh))
    z = _dot(xb, w_z)
    gate = z * jax.nn.sigmoid(z)
    y = jnp.concatenate([mix] + mem_outs, axis=-1) * gate
    out = _dot(y.astype(bf16), w_out_ref[...])
    u = DN_ALPHA * xf + out
    mu = jnp.mean(u, axis=-1, keepdims=True)
    uc = u - mu
    var = jnp.mean(uc * uc, axis=-1, keepdims=True)
    o_ref[...] = uc * lax.rsqrt(var + LN_EPS) * g_ref[...] + b_ref[...]


def _attn_layer_kernel(x_ref, w_tok_ref, w_qvt_ref, w_out_ref, kvm_ref, bias_ref, g_ref, b_ref,
                       o_ref, qt_scr, k_scr, vt_scr, s_scr, p_scr, mixt_scr):
    si = pl.program_id(1)
    xf = x_ref[...]
    xb = xf.astype(bf16)

    def shift(c):
        k_scr[:, c * TILE:(c + 1) * TILE, :] = k_scr[:, (c + 1) * TILE:(c + 2) * TILE, :]
        vt_scr[:, :, c * TILE:(c + 1) * TILE] = vt_scr[:, :, (c + 1) * TILE:(c + 2) * TILE]

    @pl.when(si == 1)
    def _():
        shift(1)

    @pl.when(si >= 2)
    def _():
        shift(0)
        shift(1)

    qt = _dot_nt(w_qvt_ref[0:E_MIX, :], xb) * (LOG2E / math.sqrt(HEAD_DIM))
    qt_scr[...] = qt.astype(bf16).reshape(N_PAIRS, LANES, TILE)
    vt = _dot_nt(w_qvt_ref[E_MIX:2 * E_MIX, :], xb)
    vt_scr[:, :, HIST:WIN] = vt.astype(bf16).reshape(N_PAIRS, LANES, TILE)
    k = _dot(xb, w_tok_ref[:, _AK0:_AK0 + E_MIX])
    for hp in range(N_PAIRS):
        k_scr[hp, HIST:WIN, :] = k[:, hp * LANES:(hp + 1) * LANES].astype(bf16)

    def attn_core(w0):
        zeros = jnp.zeros((HEAD_DIM, TILE), bf16)

        def scores(hp, e):
            lo = e * HEAD_DIM
            qth = qt_scr[hp, lo:lo + HEAD_DIM, :]
            qtm = jnp.concatenate([qth, zeros] if e == 0 else [zeros, qth], axis=0)
            st = _dot(k_scr[hp, w0:WIN, :], qtm) + bias_ref[HEADS_PER_PAIR * hp + e, w0:WIN, :]
            s_scr[e, w0:WIN, :] = st
            return jnp.max(st, axis=0, keepdims=True)

        def weighted_values(hp, e, m):
            lo = e * HEAD_DIM
            p = jnp.exp2(s_scr[e, w0:WIN, :] - m)
            l = jnp.sum(p, axis=0, keepdims=True)
            p_scr[e, w0:WIN, :] = p.astype(bf16)
            ot = _dot(vt_scr[hp, lo:lo + HEAD_DIM, w0:WIN], p_scr[e, w0:WIN, :])
            mixt_scr[hp, lo:lo + HEAD_DIM, :] = ot / l

        def pair_body(hp, m0):
            m1 = scores(hp, 1)
            weighted_values(hp, 0, m0)
            m0_next = scores(hp + 1, 0)
            weighted_values(hp, 1, m1)
            return m0_next

        m0 = lax.fori_loop(0, N_PAIRS - 1, pair_body, scores(0, 0))
        m1 = scores(N_PAIRS - 1, 1)
        weighted_values(N_PAIRS - 1, 0, m0)
        weighted_values(N_PAIRS - 1, 1, m1)

    @pl.when(si == 0)
    def _():
        attn_core(HIST)

    @pl.when(si == 1)
    def _():
        attn_core(HIST - TILE)

    @pl.when(si >= 2)
    def _():
        attn_core(0)

    mix = jnp.concatenate([mixt_scr[hp].T for hp in range(N_PAIRS)], axis=-1)
    _layer_tail(xf, xb, mix, w_tok_ref[:, _AQM0:_AQM0 + E_MEM], w_tok_ref[:, _AZ0:_AZ0 + E_BRANCH],
                w_out_ref, kvm_ref, g_ref, b_ref, o_ref)


_CARRY = 8


def _conv_layer_kernel(x_ref, w_in_ref, w_out_ref, kvm_ref, cw_ref, g_ref, b_ref, o_ref, cu_scr):
    si = pl.program_id(1)
    xf = x_ref[...]
    xb = xf.astype(bf16)

    @pl.when(si == 0)
    def _():
        cu_scr[0:_CARRY, :] = jnp.zeros((_CARRY, E_MIX), f32)

    @pl.when(si > 0)
    def _():
        cu_scr[0:_CARRY, :] = cu_scr[TILE:TILE + _CARRY, :]

    p0 = _dot(xb, w_in_ref[:, _Q0:_Q0 + E_MIX])
    p1 = _dot(xb, w_in_ref[:, _K0:_K0 + E_MIX])
    p2 = _dot(xb, w_in_ref[:, _V0:_V0 + E_MIX])
    cu_scr[_CARRY:_CARRY + TILE, :] = p1 * p2
    cw = cw_ref[...]
    conv = cu_scr[_CARRY:_CARRY + TILE, :] * cw[CONV_W - 1:CONV_W, :]
    for t in range(1, CONV_W):
        conv = conv + cu_scr[_CARRY - t:_CARRY - t + TILE, :] * cw[CONV_W - 1 - t:CONV_W - t, :]
    mix = p0 * conv
    _layer_tail(xf, xb, mix, w_in_ref[:, _QM0:_QM0 + E_MEM], w_in_ref[:, _Z0:_Z0 + E_BRANCH],
                w_out_ref, kvm_ref, g_ref, b_ref, o_ref)


def _const_spec(shape):
    nd = len(shape)
    return pl.BlockSpec(shape, lambda b, s: (0,) * nd, pipeline_mode=pl.Buffered(1))


def _layer_call(kind, x, weights, w_out_l, kvm_l, extra, g, b):
    batch, seq, _ = x.shape
    n_tiles = seq // TILE
    x_spec = pl.BlockSpec((None, TILE, D_MODEL), lambda b, s: (b, s, 0))
    kvm_spec = pl.BlockSpec((None, N_MEM, 2 * E_MEM), lambda b, s: (b, 0, 0))
    w_specs = [_const_spec(w.shape) for w in weights]
    vec_spec = _const_spec((1, D_MODEL))
    tail_specs = [_const_spec((E_BRANCH, D_MODEL)), kvm_spec, _const_spec(extra.shape),
                  vec_spec, vec_spec]
    if kind == "attn":
        body = _attn_layer_kernel
        scratch = [
            pltpu.VMEM((N_PAIRS, LANES, TILE), bf16),
            pltpu.VMEM((N_PAIRS, WIN, LANES), bf16),
            pltpu.VMEM((N_PAIRS, LANES, WIN), bf16),
            pltpu.VMEM((HEADS_PER_PAIR, WIN, TILE), f32),
            pltpu.VMEM((HEADS_PER_PAIR, WIN, TILE), bf16),
            pltpu.VMEM((N_PAIRS, LANES, TILE), f32),
        ]
    else:
        body = _conv_layer_kernel
        scratch = [pltpu.VMEM((TILE + 2 * _CARRY, E_MIX), f32)]
    return pl.pallas_call(
        body,
        grid=(batch, n_tiles),
        in_specs=[x_spec] + w_specs + tail_specs,
        out_specs=pl.BlockSpec((None, TILE, D_MODEL), lambda b, s: (b, s, 0)),
        out_shape=jax.ShapeDtypeStruct(x.shape, f32),
        scratch_shapes=scratch,
        compiler_params=pltpu.CompilerParams(
            dimension_semantics=("arbitrary", "arbitrary"), vmem_limit_bytes=VMEM_LIMIT),
        name=f"{kind}_layer",
    )(x, *weights, w_out_l, kvm_l, extra, g, b)


def kernel(x, mem, w_in, w_mem_kv, w_out, rel_bias, conv_w, ln_g, ln_b):
    batch, seq, d = x.shape
    assert d == D_MODEL and seq % TILE == 0 and mem.shape == (batch, N_MEM, D_MODEL)
    w_in_bf = w_in.astype(bf16)
    w_out_bf = w_out.astype(bf16)
    kvm = _kv_mem_all(mem.reshape(batch * N_MEM, D_MODEL), w_mem_kv.astype(bf16))
    kvm = kvm.reshape(DEPTH, batch, N_MEM, 2 * E_MEM)
    table_pad = jnp.pad(rel_bias, ((0, 0), (0, 0), (0, REL_PAD - N_REL)))
    for layer in range(DEPTH):
        g = ln_g[layer].reshape(1, D_MODEL)
        b = ln_b[layer].reshape(1, D_MODEL)
        w_l = w_in_bf[layer]
        if layer % 2 == 0:
            w_tok = jnp.concatenate([w_l[:, _K0:_K0 + E_MIX], w_l[:, _QM0:]], axis=1)
            w_qvt = jnp.concatenate([w_l[:, _Q0:_Q0 + E_MIX], w_l[:, _V0:_V0 + E_MIX]], axis=1).T
            x = _layer_call("attn", x, [w_tok, w_qvt], w_out_bf[layer], kvm[layer],
                            _tile_bias(table_pad[layer // 2]), g, b)
        else:
            x = _layer_call("conv", x, [w_l], w_out_bf[layer], kvm[layer],
                            conv_w[layer // 2], g, b)
    return x
```

```python
import math

import jax
import jax.numpy as jnp
from jax import lax
from jax.experimental import pallas as pl
from jax.experimental.pallas import tpu as pltpu

D_MODEL = 1024
DEPTH = 4
CHUNK = 64
N_PREV = 8
N_HEADS = 16
HEAD_DIM = 64
E_MIX = N_HEADS * HEAD_DIM
REL_CLIP = 128
N_REL = 2 * REL_CLIP + 1
CONV_W = 3
N_MEM = 256
MEM_HEADS = 4
MEM_HEAD_DIM = 128
E_MEM = MEM_HEADS * MEM_HEAD_DIM
E_BRANCH = E_MIX + E_MEM
N_IN = 3 * E_MIX + E_MEM + E_BRANCH
DN_ALPHA = (2.0 * DEPTH) ** 0.25
LN_EPS = 1e-5

LANES = 128
TILE = 256
HIST = N_PREV * CHUNK
WIN = HIST + TILE
N_PAIRS = E_MIX // LANES
HEADS_PER_PAIR = LANES // HEAD_DIM
ONES_ROWS = 16
VT_ROWS = LANES + 2 * ONES_ROWS
REL_PAD = 384
BASE_W = 1024
NEG = -1e30
LOG2E = math.log2(math.e)
VMEM_LIMIT = 56 * 1024 * 1024

_Q0, _K0, _V0 = 0, E_MIX, 2 * E_MIX
_QM0 = 3 * E_MIX
_Z0 = 3 * E_MIX + E_MEM
_AK0, _AQM0, _AZ0 = 0, E_MIX, E_MIX + E_MEM

bf16 = jnp.bfloat16
f32 = jnp.float32


def _dot(a, b):
    return jnp.dot(a, b, preferred_element_type=f32)


def _dot_nt(a, b):
    return lax.dot_general(a, b, (((1,), (1,)), ((), ())), preferred_element_type=f32)


def _softmax2_pv(s, v):
    m = jnp.max(s, axis=-1, keepdims=True)
    p = jnp.exp2(s - m)
    l = jnp.sum(p, axis=-1, keepdims=True)
    return _dot(p.astype(bf16), v) / l


def _kvmem_kernel(mem_ref, w_ref, o_ref):
    o_ref[...] = _dot(mem_ref[...].astype(bf16), w_ref[...]).astype(bf16)


def _kv_mem_all(mem2d, w_mem_kv_bf):
    rows = mem2d.shape[0]
    return pl.pallas_call(
        _kvmem_kernel,
        grid=(DEPTH,),
        in_specs=[
            pl.BlockSpec((rows, D_MODEL), lambda l: (0, 0)),
            pl.BlockSpec((None, D_MODEL, 2 * E_MEM), lambda l: (l, 0, 0)),
        ],
        out_specs=pl.BlockSpec((None, rows, 2 * E_MEM), lambda l: (l, 0, 0)),
        out_shape=jax.ShapeDtypeStruct((DEPTH, rows, 2 * E_MEM), bf16),
        compiler_params=pltpu.CompilerParams(
            dimension_semantics=("arbitrary",), vmem_limit_bytes=VMEM_LIMIT),
        name="kv_mem_proj",
    )(mem2d, w_mem_kv_bf)


_BASE_OFF = TILE - 1


def _bias_kernel(table_ref, o_ref):
    t = table_ref[...]
    c = lax.broadcasted_iota(jnp.int32, (REL_PAD, BASE_W), 1)
    j = lax.broadcasted_iota(jnp.int32, (REL_PAD, BASE_W), 0)
    idx = jnp.clip(c - _BASE_OFF, -REL_CLIP, REL_CLIP) + REL_CLIP
    onehot = jnp.where(j == idx, 1.0, 0.0).astype(bf16)
    hi = t.astype(bf16)
    r1 = t - hi.astype(f32)
    mid = r1.astype(bf16)
    lo = (r1 - mid.astype(f32)).astype(bf16)
    base = (_dot(hi, onehot) + _dot(mid, onehot) + _dot(lo, onehot)) * LOG2E

    qchunk = lax.broadcasted_iota(jnp.int32, (N_HEADS, TILE), 1) // CHUNK

    def row(p, carry):
        rolled = pltpu.roll(base, p + (BASE_W - HIST - _BASE_OFF), 1)[:, 0:TILE]
        d = p // CHUNK - qchunk
        val = jnp.where((d >= 0) & (d <= N_PREV), rolled, NEG)
        o_ref[:, pl.ds(p, 1), :] = val[:, None, :]
        return carry

    lax.fori_loop(0, WIN, row, 0)


def _tile_bias(table_pad):
    return pl.pallas_call(
        _bias_kernel,
        out_shape=jax.ShapeDtypeStruct((N_HEADS, WIN, TILE), f32),
        compiler_params=pltpu.CompilerParams(vmem_limit_bytes=VMEM_LIMIT),
        name="rel_bias_tile",
    )(table_pad)


def _layer_tail(xf, xb, mix, w_qm, w_z, w_out_ref, kvm_ref, g_ref, b_ref, o_ref):
    qm = _dot(xb, w_qm) * (LOG2E / math.sqrt(MEM_HEAD_DIM))
    mem_outs = []
    for h in range(MEM_HEADS):
        lo = h * MEM_HEAD_DIM
        qh = qm[:, lo:lo + MEM_HEAD_DIM].astype(bf16)
        kh = kvm_ref[:, lo:lo + MEM_HEAD_DIM]
        vh = kvm_ref[:, E_MEM + lo:E_MEM + lo + MEM_HEAD_DIM]
        mem_outs.append(_softmax2_pv(_dot_nt(qh, kh), vh))
    z = _dot(xb, w_z)
    gate = z * jax.nn.sigmoid(z)
    y = jnp.concatenate([mix] + mem_outs, axis=-1) * gate
    out = _dot(y.astype(bf16), w_out_ref[...])
    u = DN_ALPHA * xf + out
    mu = jnp.mean(u, axis=-1, keepdims=True)
    uc = u - mu
    var = jnp.mean(uc * uc, axis=-1, keepdims=True)
    o_ref[...] = uc * lax.rsqrt(var + LN_EPS) * g_ref[...] + b_ref[...]


def _attn_layer_kernel(x_ref, w_tok_ref, w_qvt_ref, w_out_ref, kvm_ref, bias_ref, g_ref, b_ref,
                       o_ref, qt_scr, k_scr, vt_scr, s_scr, p_scr, mixt_scr):
    si = pl.program_id(1)
    xf = x_ref[...]
    xb = xf.astype(bf16)

    def shift(c):
        k_scr[:, c * TILE:(c + 1) * TILE, :] = k_scr[:, (c + 1) * TILE:(c + 2) * TILE, :]
        vt_scr[:, :, c * TILE:(c + 1) * TILE] = vt_scr[:, :, (c + 1) * TILE:(c + 2) * TILE]

    @pl.when(si == 1)
    def _():
        shift(1)

    @pl.when(si >= 2)
    def _():
        shift(0)
        shift(1)

    qt = _dot_nt(w_qvt_ref[0:E_MIX, :], xb) * (LOG2E / math.sqrt(HEAD_DIM))
    qt_scr[...] = qt.astype(bf16).reshape(N_PAIRS, LANES, TILE)
    vt = _dot_nt(w_qvt_ref[E_MIX:2 * E_MIX, :], xb)
    vt_scr[:, ONES_ROWS:ONES_ROWS + LANES, HIST:WIN] = vt.astype(bf16).reshape(N_PAIRS, LANES, TILE)
    ones = jnp.ones((N_PAIRS, ONES_ROWS, TILE), bf16)
    vt_scr[:, 0:ONES_ROWS, HIST:WIN] = ones
    vt_scr[:, ONES_ROWS + LANES:VT_ROWS, HIST:WIN] = ones
    k = _dot(xb, w_tok_ref[:, _AK0:_AK0 + E_MIX])
    for hp in range(N_PAIRS):
        k_scr[hp, HIST:WIN, :] = k[:, hp * LANES:(hp + 1) * LANES].astype(bf16)

    def attn_core(w0):
        zeros = jnp.zeros((HEAD_DIM, TILE), bf16)

        def scores(hp, e):
            lo = e * HEAD_DIM
            qth = qt_scr[hp, lo:lo + HEAD_DIM, :]
            qtm = jnp.concatenate([qth, zeros] if e == 0 else [zeros, qth], axis=0)
            mid = (w0 + WIN) // 2
            m = None
            for a, b in ((w0, mid), (mid, WIN)):
                st = _dot(k_scr[hp, a:b, :], qtm) + bias_ref[HEADS_PER_PAIR * hp + e, a:b, :]
                s_scr[e, a:b, :] = st
                mh = jnp.max(st, axis=0, keepdims=True)
                m = mh if m is None else jnp.maximum(m, mh)
            return m

        def exponentials(e, m):
            p_scr[e, w0:WIN, :] = jnp.exp2(s_scr[e, w0:WIN, :] - m).astype(bf16)

        def weighted_values(hp, e):
            lo = e * HEAD_DIM
            r0 = e * (HEAD_DIM + ONES_ROWS)
            ot = _dot(vt_scr[hp, r0:r0 + HEAD_DIM + ONES_ROWS, w0:WIN], p_scr[e, w0:WIN, :])
            if e == 0:
                l, num = ot[0:1, :], ot[ONES_ROWS:, :]
            else:
                l, num = ot[HEAD_DIM:HEAD_DIM + 1, :], ot[0:HEAD_DIM, :]
            mixt_scr[hp, lo:lo + HEAD_DIM, :] = num / l

        def pair_body(hp, m1):
            weighted_values(hp, 0)
            m0_next = scores(hp + 1, 0)
            exponentials(1, m1)
            weighted_values(hp, 1)
            m1_next = scores(hp + 1, 1)
            exponentials(0, m0_next)
            return m1_next

        m0 = scores(0, 0)
        m1 = scores(0, 1)
        exponentials(0, m0)
        for hp in range(N_PAIRS - 1):
            m1 = pair_body(hp, m1)
        weighted_values(N_PAIRS - 1, 0)
        exponentials(1, m1)
        weighted_values(N_PAIRS - 1, 1)

    @pl.when(si == 0)
    def _():
        attn_core(HIST)

    @pl.when(si == 1)
    def _():
        attn_core(HIST - TILE)

    @pl.when(si >= 2)
    def _():
        attn_core(0)

    mix = jnp.concatenate([mixt_scr[hp].T for hp in range(N_PAIRS)], axis=-1)
    _layer_tail(xf, xb, mix, w_tok_ref[:, _AQM0:_AQM0 + E_MEM], w_tok_ref[:, _AZ0:_AZ0 + E_BRANCH],
                w_out_ref, kvm_ref, g_ref, b_ref, o_ref)


_CARRY = 8


def _conv_layer_kernel(x_ref, w_in_ref, w_out_ref, kvm_ref, cw_ref, g_ref, b_ref, o_ref, cu_scr):
    si = pl.program_id(1)
    xf = x_ref[...]
    xb = xf.astype(bf16)

    @pl.when(si == 0)
    def _():
        cu_scr[0:_CARRY, :] = jnp.zeros((_CARRY, E_MIX), f32)

    @pl.when(si > 0)
    def _():
        cu_scr[0:_CARRY, :] = cu_scr[TILE:TILE + _CARRY, :]

    p0 = _dot(xb, w_in_ref[:, _Q0:_Q0 + E_MIX])
    p1 = _dot(xb, w_in_ref[:, _K0:_K0 + E_MIX])
    p2 = _dot(xb, w_in_ref[:, _V0:_V0 + E_MIX])
    cu_scr[_CARRY:_CARRY + TILE, :] = p1 * p2
    cw = cw_ref[...]
    conv = cu_scr[_CARRY:_CARRY + TILE, :] * cw[CONV_W - 1:CONV_W, :]
    for t in range(1, CONV_W):
        conv = conv + cu_scr[_CARRY - t:_CARRY - t + TILE, :] * cw[CONV_W - 1 - t:CONV_W - t, :]
    mix = p0 * conv
    _layer_tail(xf, xb, mix, w_in_ref[:, _QM0:_QM0 + E_MEM], w_in_ref[:, _Z0:_Z0 + E_BRANCH],
                w_out_ref, kvm_ref, g_ref, b_ref, o_ref)


def _const_spec(shape):
    nd = len(shape)
    return pl.BlockSpec(shape, lambda b, s: (0,) * nd, pipeline_mode=pl.Buffered(1))


def _layer_call(kind, x, weights, w_out_l, kvm_l, extra, g, b):
    batch, seq, _ = x.shape
    n_tiles = seq // TILE
    x_spec = pl.BlockSpec((None, TILE, D_MODEL), lambda b, s: (b, s, 0))
    kvm_spec = pl.BlockSpec((None, N_MEM, 2 * E_MEM), lambda b, s: (b, 0, 0))
    w_specs = [_const_spec(w.shape) for w in weights]
    vec_spec = _const_spec((1, D_MODEL))
    tail_specs = [_const_spec((E_BRANCH, D_MODEL)), kvm_spec, _const_spec(extra.shape),
                  vec_spec, vec_spec]
    if kind == "attn":
        body = _attn_layer_kernel
        scratch = [
            pltpu.VMEM((N_PAIRS, LANES, TILE), bf16),
            pltpu.VMEM((N_PAIRS, WIN, LANES), bf16),
            pltpu.VMEM((N_PAIRS, VT_ROWS, WIN), bf16),
            pltpu.VMEM((HEADS_PER_PAIR, WIN, TILE), f32),
            pltpu.VMEM((HEADS_PER_PAIR, WIN, TILE), bf16),
            pltpu.VMEM((N_PAIRS, LANES, TILE), f32),
        ]
    else:
        body = _conv_layer_kernel
        scratch = [pltpu.VMEM((TILE + 2 * _CARRY, E_MIX), f32)]
    return pl.pallas_call(
        body,
        grid=(batch, n_tiles),
        in_specs=[x_spec] + w_specs + tail_specs,
        out_specs=pl.BlockSpec((None, TILE, D_MODEL), lambda b, s: (b, s, 0)),
        out_shape=jax.ShapeDtypeStruct(x.shape, f32),
        scratch_shapes=scratch,
        compiler_params=pltpu.CompilerParams(
            dimension_semantics=("arbitrary", "arbitrary"), vmem_limit_bytes=VMEM_LIMIT),
        name=f"{kind}_layer",
    )(x, *weights, w_out_l, kvm_l, extra, g, b)


def kernel(x, mem, w_in, w_mem_kv, w_out, rel_bias, conv_w, ln_g, ln_b):
    batch, seq, d = x.shape
    assert d == D_MODEL and seq % TILE == 0 and mem.shape == (batch, N_MEM, D_MODEL)
    w_in_bf = w_in.astype(bf16)
    w_out_bf = w_out.astype(bf16)
    kvm = _kv_mem_all(mem.reshape(batch * N_MEM, D_MODEL), w_mem_kv.astype(bf16))
    kvm = kvm.reshape(DEPTH, batch, N_MEM, 2 * E_MEM)
    table_pad = jnp.pad(rel_bias, ((0, 0), (0, 0), (0, REL_PAD - N_REL)))
    for layer in range(DEPTH):
        g = ln_g[layer].reshape(1, D_MODEL)
        b = ln_b[layer].reshape(1, D_MODEL)
        w_l = w_in_bf[layer]
        if layer % 2 == 0:
            w_tok = jnp.concatenate([w_l[:, _K0:_K0 + E_MIX], w_l[:, _QM0:]], axis=1)
            w_qvt = jnp.concatenate([w_l[:, _Q0:_Q0 + E_MIX], w_l[:, _V0:_V0 + E_MIX]], axis=1).T
            x = _layer_call("attn", x, [w_tok, w_qvt], w_out_bf[layer], kvm[layer],
                            _tile_bias(table_pad[layer // 2]), g, b)
        else:
            x = _layer_call("conv", x, [w_l], w_out_bf[layer], kvm[layer],
                            conv_w[layer // 2], g, b)
    return x
```

```python
import math

import jax
import jax.numpy as jnp
from jax import lax
from jax.experimental import pallas as pl
from jax.experimental.pallas import tpu as pltpu

D_MODEL = 1024
DEPTH = 4
CHUNK = 64
N_PREV = 8
N_HEADS = 16
HEAD_DIM = 64
E_MIX = N_HEADS * HEAD_DIM
REL_CLIP = 128
N_REL = 2 * REL_CLIP + 1
CONV_W = 3
N_MEM = 256
MEM_HEADS = 4
MEM_HEAD_DIM = 128
E_MEM = MEM_HEADS * MEM_HEAD_DIM
E_BRANCH = E_MIX + E_MEM
N_IN = 3 * E_MIX + E_MEM + E_BRANCH
DN_ALPHA = (2.0 * DEPTH) ** 0.25
LN_EPS = 1e-5

LANES = 128
TILE = 256
HIST = N_PREV * CHUNK
WIN = HIST + TILE
N_PAIRS = E_MIX // LANES
HEADS_PER_PAIR = LANES // HEAD_DIM
ONES_ROWS = 16
VT_ROWS = LANES + 2 * ONES_ROWS
HALF = TILE // 2
HALF_WIN = HIST + HALF
BASE_W = 384
NEG = -1e30
LOG2E = math.log2(math.e)
VMEM_LIMIT = 56 * 1024 * 1024

_Q0, _K0, _V0 = 0, E_MIX, 2 * E_MIX
_QM0 = 3 * E_MIX
_Z0 = 3 * E_MIX + E_MEM
_AK0, _AQM0, _AZ0 = 0, E_MIX, E_MIX + E_MEM

bf16 = jnp.bfloat16
f32 = jnp.float32


def _dot(a, b):
    return jnp.dot(a, b, preferred_element_type=f32)


def _dot_nt(a, b):
    return lax.dot_general(a, b, (((1,), (1,)), ((), ())), preferred_element_type=f32)


def _softmax2_pv(s, v):
    m = jnp.max(s, axis=-1, keepdims=True)
    p = jnp.exp2(s - m)
    l = jnp.sum(p, axis=-1, keepdims=True)
    return _dot(p.astype(bf16), v) / l


def _kvmem_kernel(mem_ref, w_ref, o_ref):
    o_ref[...] = _dot(mem_ref[...].astype(bf16), w_ref[...]).astype(bf16)


def _kv_mem_all(mem2d, w_mem_kv_bf):
    rows = mem2d.shape[0]
    return pl.pallas_call(
        _kvmem_kernel,
        grid=(DEPTH,),
        in_specs=[
            pl.BlockSpec((rows, D_MODEL), lambda l: (0, 0)),
            pl.BlockSpec((None, D_MODEL, 2 * E_MEM), lambda l: (l, 0, 0)),
        ],
        out_specs=pl.BlockSpec((None, rows, 2 * E_MEM), lambda l: (l, 0, 0)),
        out_shape=jax.ShapeDtypeStruct((DEPTH, rows, 2 * E_MEM), bf16),
        compiler_params=pltpu.CompilerParams(
            dimension_semantics=("arbitrary",), vmem_limit_bytes=VMEM_LIMIT),
        name="kv_mem_proj",
    )(mem2d, w_mem_kv_bf)


_FAR_ROWS = HIST - REL_CLIP


def _bias_kernel(tshift_ref, o_ref):
    c = lax.broadcasted_iota(jnp.int32, (N_HEADS, BASE_W), 1)
    ts = tshift_ref[...]
    last = ts[:, 2 * REL_CLIP - 1:2 * REL_CLIP]
    base = jnp.where(c >= 2 * REL_CLIP, last, ts) * LOG2E
    far = base[:, BASE_W - HALF:]

    qchunk = lax.broadcasted_iota(jnp.int32, (N_HEADS, HALF), 1) // CHUNK
    d_far = (lax.broadcasted_iota(jnp.int32, (N_HEADS, _FAR_ROWS, HALF), 1) // CHUNK
             - lax.broadcasted_iota(jnp.int32, (N_HEADS, _FAR_ROWS, HALF), 2) // CHUNK)
    o_ref[:, 0:_FAR_ROWS, :] = jnp.where(d_far >= 0, far[:, None, :], NEG)

    def row(p, carry):
        rolled = pltpu.roll(base, (p + (BASE_W - HALF_WIN + 1)) % BASE_W, 1)[:, 0:HALF]
        val = jnp.where(p // CHUNK - qchunk <= N_PREV, rolled, NEG)
        o_ref[:, pl.ds(p, 1), :] = val[:, None, :]
        return carry

    lax.fori_loop(_FAR_ROWS, HALF_WIN, row, 0)


def _tile_bias(tshift):
    return pl.pallas_call(
        _bias_kernel,
        out_shape=jax.ShapeDtypeStruct((N_HEADS, HALF_WIN, HALF), f32),
        compiler_params=pltpu.CompilerParams(vmem_limit_bytes=VMEM_LIMIT),
        name="rel_bias_tile",
    )(tshift)


def _layer_tail(xf, xb, mix, w_qm, w_z, w_out_ref, kvm_ref, g_ref, b_ref, o_ref):
    qm = _dot(xb, w_qm) * (LOG2E / math.sqrt(MEM_HEAD_DIM))
    mem_outs = []
    for h in range(MEM_HEADS):
        lo = h * MEM_HEAD_DIM
        qh = qm[:, lo:lo + MEM_HEAD_DIM].astype(bf16)
        kh = kvm_ref[:, lo:lo + MEM_HEAD_DIM]
        vh = kvm_ref[:, E_MEM + lo:E_MEM + lo + MEM_HEAD_DIM]
        mem_outs.append(_softmax2_pv(_dot_nt(qh, kh), vh))
    z = _dot(xb, w_z)
    gate = z * jax.nn.sigmoid(z)
    y = jnp.concatenate([mix] + mem_outs, axis=-1) * gate
    out = _dot(y.astype(bf16), w_out_ref[...])
    u = DN_ALPHA * xf + out
    mu = jnp.mean(u, axis=-1, keepdims=True)
    uc = u - mu
    var = jnp.mean(uc * uc, axis=-1, keepdims=True)
    o_ref[...] = uc * lax.rsqrt(var + LN_EPS) * g_ref[...] + b_ref[...]


def _attn_layer_kernel(x_ref, w_tok_ref, w_qvt_ref, w_out_ref, kvm_ref, bias_ref, g_ref, b_ref,
                       o_ref, qt_scr, k_scr, vt_scr, s_scr, p_scr, mixt_scr):
    si = pl.program_id(1)
    xf = x_ref[...]
    xb = xf.astype(bf16)

    def shift(c):
        k_scr[:, c * TILE:(c + 1) * TILE, :] = k_scr[:, (c + 1) * TILE:(c + 2) * TILE, :]
        vt_scr[:, :, c * TILE:(c + 1) * TILE] = vt_scr[:, :, (c + 1) * TILE:(c + 2) * TILE]

    @pl.when(si == 1)
    def _():
        shift(1)

    @pl.when(si >= 2)
    def _():
        shift(0)
        shift(1)

    qt = _dot_nt(w_qvt_ref[0:E_MIX, :], xb) * (LOG2E / math.sqrt(HEAD_DIM))
    qt_scr[...] = qt.astype(bf16).reshape(N_PAIRS, LANES, TILE)
    vt = _dot_nt(w_qvt_ref[E_MIX:2 * E_MIX, :], xb)
    vt_scr[:, ONES_ROWS:ONES_ROWS + LANES, HIST:WIN] = vt.astype(bf16).reshape(N_PAIRS, LANES, TILE)
    ones = jnp.ones((N_PAIRS, ONES_ROWS, TILE), bf16)
    vt_scr[:, 0:ONES_ROWS, HIST:WIN] = ones
    vt_scr[:, ONES_ROWS + LANES:VT_ROWS, HIST:WIN] = ones
    k = _dot(xb, w_tok_ref[:, _AK0:_AK0 + E_MIX])
    for hp in range(N_PAIRS):
        k_scr[hp, HIST:WIN, :] = k[:, hp * LANES:(hp + 1) * LANES].astype(bf16)

    def attn_core(w0):
        zeros = jnp.zeros((HEAD_DIM, TILE), bf16)

        def scores(hp, e):
            lo = e * HEAD_DIM
            qth = qt_scr[hp, lo:lo + HEAD_DIM, :]
            qtm = jnp.concatenate([qth, zeros] if e == 0 else [zeros, qth], axis=0)
            h = HEADS_PER_PAIR * hp + e
            mid = (w0 + WIN) // 2
            ms = [None, None]
            for a, b in ((w0, mid), (mid, WIN)):
                st = _dot(k_scr[hp, a:b, :], qtm)
                for half in range(2):
                    ra, rb = max(a, half * HALF), min(b, half * HALF + HALF_WIN)
                    if rb <= ra:
                        continue
                    cols = slice(half * HALF, (half + 1) * HALF)
                    sh = st[ra - a:rb - a, cols] + bias_ref[h, ra - half * HALF:rb - half * HALF, :]
                    s_scr[e, ra:rb, cols] = sh
                    mh = jnp.max(sh, axis=0, keepdims=True)
                    ms[half] = mh if ms[half] is None else jnp.maximum(ms[half], mh)
            return ms

        def exponentials(e, ms):
            for half in range(2):
                ra, rb = max(w0, half * HALF), half * HALF + HALF_WIN
                cols = slice(half * HALF, (half + 1) * HALF)
                p_scr[e, ra:rb, cols] = jnp.exp2(s_scr[e, ra:rb, cols] - ms[half]).astype(bf16)

        def weighted_values(hp, e):
            lo = e * HEAD_DIM
            r0 = e * (HEAD_DIM + ONES_ROWS)
            ot = _dot(vt_scr[hp, r0:r0 + HEAD_DIM + ONES_ROWS, w0:WIN], p_scr[e, w0:WIN, :])
            if e == 0:
                l, num = ot[0:1, :], ot[ONES_ROWS:, :]
            else:
                l, num = ot[HEAD_DIM:HEAD_DIM + 1, :], ot[0:HEAD_DIM, :]
            mixt_scr[hp, lo:lo + HEAD_DIM, :] = num / l

        def pair_body(hp, m1):
            weighted_values(hp, 0)
            m0_next = scores(hp + 1, 0)
            exponentials(1, m1)
            weighted_values(hp, 1)
            m1_next = scores(hp + 1, 1)
            exponentials(0, m0_next)
            return m1_next

        m0 = scores(0, 0)
        m1 = scores(0, 1)
        exponentials(0, m0)
        for hp in range(N_PAIRS - 1):
            m1 = pair_body(hp, m1)
        weighted_values(N_PAIRS - 1, 0)
        exponentials(1, m1)
        weighted_values(N_PAIRS - 1, 1)

    @pl.when(si == 0)
    def _():
        p_scr[:, HALF_WIN:WIN, 0:HALF] = jnp.zeros((HEADS_PER_PAIR, WIN - HALF_WIN, HALF), bf16)
        p_scr[:, 0:HALF, HALF:TILE] = jnp.zeros((HEADS_PER_PAIR, HALF, HALF), bf16)
        attn_core(HIST)

    @pl.when(si == 1)
    def _():
        attn_core(HIST - TILE)

    @pl.when(si >= 2)
    def _():
        attn_core(0)

    mix = jnp.concatenate([mixt_scr[hp].T for hp in range(N_PAIRS)], axis=-1)
    _layer_tail(xf, xb, mix, w_tok_ref[:, _AQM0:_AQM0 + E_MEM], w_tok_ref[:, _AZ0:_AZ0 + E_BRANCH],
                w_out_ref, kvm_ref, g_ref, b_ref, o_ref)


_CARRY = 8


def _conv_layer_kernel(x_ref, w_in_ref, w_out_ref, kvm_ref, cw_ref, g_ref, b_ref, o_ref, cu_scr):
    si = pl.program_id(1)
    xf = x_ref[...]
    xb = xf.astype(bf16)

    @pl.when(si == 0)
    def _():
        cu_scr[0:_CARRY, :] = jnp.zeros((_CARRY, E_MIX), f32)

    @pl.when(si > 0)
    def _():
        cu_scr[0:_CARRY, :] = cu_scr[TILE:TILE + _CARRY, :]

    p0 = _dot(xb, w_in_ref[:, _Q0:_Q0 + E_MIX])
    p1 = _dot(xb, w_in_ref[:, _K0:_K0 + E_MIX])
    p2 = _dot(xb, w_in_ref[:, _V0:_V0 + E_MIX])
    cu_scr[_CARRY:_CARRY + TILE, :] = p1 * p2
    cw = cw_ref[...]
    conv = cu_scr[_CARRY:_CARRY + TILE, :] * cw[CONV_W - 1:CONV_W, :]
    for t in range(1, CONV_W):
        conv = conv + cu_scr[_CARRY - t:_CARRY - t + TILE, :] * cw[CONV_W - 1 - t:CONV_W - t, :]
    mix = p0 * conv
    _layer_tail(xf, xb, mix, w_in_ref[:, _QM0:_QM0 + E_MEM], w_in_ref[:, _Z0:_Z0 + E_BRANCH],
                w_out_ref, kvm_ref, g_ref, b_ref, o_ref)


def _const_spec(shape):
    nd = len(shape)
    return pl.BlockSpec(shape, lambda b, s: (0,) * nd, pipeline_mode=pl.Buffered(1))


def _layer_call(kind, x, weights, w_out_l, kvm_l, extra, g, b):
    batch, seq, _ = x.shape
    n_tiles = seq // TILE
    x_spec = pl.BlockSpec((None, TILE, D_MODEL), lambda b, s: (b, s, 0))
    kvm_spec = pl.BlockSpec((None, N_MEM, 2 * E_MEM), lambda b, s: (b, 0, 0))
    w_specs = [_const_spec(w.shape) for w in weights]
    vec_spec = _const_spec((1, D_MODEL))
    tail_specs = [_const_spec((E_BRANCH, D_MODEL)), kvm_spec, _const_spec(extra.shape),
                  vec_spec, vec_spec]
    if kind == "attn":
        body = _attn_layer_kernel
        scratch = [
            pltpu.VMEM((N_PAIRS, LANES, TILE), bf16),
            pltpu.VMEM((N_PAIRS, WIN, LANES), bf16),
            pltpu.VMEM((N_PAIRS, VT_ROWS, WIN), bf16),
            pltpu.VMEM((HEADS_PER_PAIR, WIN, TILE), f32),
            pltpu.VMEM((HEADS_PER_PAIR, WIN, TILE), bf16),
            pltpu.VMEM((N_PAIRS, LANES, TILE), f32),
        ]
    else:
        body = _conv_layer_kernel
        scratch = [pltpu.VMEM((TILE + 2 * _CARRY, E_MIX), f32)]
    return pl.pallas_call(
        body,
        grid=(batch, n_tiles),
        in_specs=[x_spec] + w_specs + tail_specs,
        out_specs=pl.BlockSpec((None, TILE, D_MODEL), lambda b, s: (b, s, 0)),
        out_shape=jax.ShapeDtypeStruct(x.shape, f32),
        scratch_shapes=scratch,
        compiler_params=pltpu.CompilerParams(
            dimension_semantics=("arbitrary", "arbitrary"), vmem_limit_bytes=VMEM_LIMIT),
        name=f"{kind}_layer",
    )(x, *weights, w_out_l, kvm_l, extra, g, b)


def kernel(x, mem, w_in, w_mem_kv, w_out, rel_bias, conv_w, ln_g, ln_b):
    batch, seq, d = x.shape
    assert d == D_MODEL and seq % TILE == 0 and mem.shape == (batch, N_MEM, D_MODEL)
    w_in_bf = w_in.astype(bf16)
    w_out_bf = w_out.astype(bf16)
    kvm = _kv_mem_all(mem.reshape(batch * N_MEM, D_MODEL), w_mem_kv.astype(bf16))
    kvm = kvm.reshape(DEPTH, batch, N_MEM, 2 * E_MEM)
    tshift = jnp.pad(rel_bias[:, :, 1:], ((0, 0), (0, 0), (0, BASE_W - (N_REL - 1))))
    for layer in range(DEPTH):
        g = ln_g[layer].reshape(1, D_MODEL)
        b = ln_b[layer].reshape(1, D_MODEL)
        w_l = w_in_bf[layer]
        if layer % 2 == 0:
            w_tok = jnp.concatenate([w_l[:, _K0:_K0 + E_MIX], w_l[:, _QM0:]], axis=1)
            w_qvt = jnp.concatenate([w_l[:, _Q0:_Q0 + E_MIX], w_l[:, _V0:_V0 + E_MIX]], axis=1).T
            x = _layer_call("attn", x, [w_tok, w_qvt], w_out_bf[layer], kvm[layer],
                            _tile_bias(tshift[layer // 2]), g, b)
        else:
            x = _layer_call("conv", x, [w_l], w_out_bf[layer], kvm[layer],
                            conv_w[layer // 2], g, b)
    return x
```

```python
import functools
import math

import jax
import jax.numpy as jnp
from jax import lax
from jax.experimental import pallas as pl
from jax.experimental.pallas import tpu as pltpu

D_MODEL = 1024
DEPTH = 4
CHUNK = 64
N_PREV = 8
N_HEADS = 16
HEAD_DIM = 64
E_MIX = N_HEADS * HEAD_DIM
REL_CLIP = 128
N_REL = 2 * REL_CLIP + 1
CONV_W = 3
N_MEM = 256
MEM_HEADS = 4
MEM_HEAD_DIM = 128
E_MEM = MEM_HEADS * MEM_HEAD_DIM
E_BRANCH = E_MIX + E_MEM
N_IN = 3 * E_MIX + E_MEM + E_BRANCH
DN_ALPHA = (2.0 * DEPTH) ** 0.25
LN_EPS = 1e-5

LANES = 128
TILE = 256
HIST = N_PREV * CHUNK
WIN = HIST + TILE
N_PAIRS = E_MIX // LANES
HEADS_PER_PAIR = LANES // HEAD_DIM
ONES_ROWS = 16
VT_ROWS = LANES + 2 * ONES_ROWS
HALF = TILE // 2
HALF_WIN = HIST + HALF
BASE_W = 384
NEG = -1e30
LOG2E = math.log2(math.e)
VMEM_LIMIT = 56 * 1024 * 1024

_Q0, _K0, _V0 = 0, E_MIX, 2 * E_MIX
_QM0 = 3 * E_MIX
_Z0 = 3 * E_MIX + E_MEM
_AK0, _AQM0, _AZ0 = 0, E_MIX, E_MIX + E_MEM

bf16 = jnp.bfloat16
f32 = jnp.float32


def _dot(a, b):
    return jnp.dot(a, b, preferred_element_type=f32)


def _dot_nt(a, b):
    return lax.dot_general(a, b, (((1,), (1,)), ((), ())), preferred_element_type=f32)


def _softmax2_pv(s, v):
    m = jnp.max(s, axis=-1, keepdims=True)
    p = jnp.exp2(s - m)
    l = jnp.sum(p, axis=-1, keepdims=True)
    return _dot(p.astype(bf16), v) / l


def _kvmem_kernel(mem_ref, w_ref, o_ref):
    o_ref[...] = _dot(mem_ref[...].astype(bf16), w_ref[...]).astype(bf16)


def _kv_mem_all(mem2d, w_mem_kv_bf):
    rows = mem2d.shape[0]
    return pl.pallas_call(
        _kvmem_kernel,
        grid=(DEPTH,),
        in_specs=[
            pl.BlockSpec((rows, D_MODEL), lambda l: (0, 0)),
            pl.BlockSpec((None, D_MODEL, 2 * E_MEM), lambda l: (l, 0, 0)),
        ],
        out_specs=pl.BlockSpec((None, rows, 2 * E_MEM), lambda l: (l, 0, 0)),
        out_shape=jax.ShapeDtypeStruct((DEPTH, rows, 2 * E_MEM), bf16),
        compiler_params=pltpu.CompilerParams(
            dimension_semantics=("arbitrary",), vmem_limit_bytes=VMEM_LIMIT),
        name="kv_mem_proj",
    )(mem2d, w_mem_kv_bf)


_FAR_ROWS = HIST - REL_CLIP


def _bias_kernel(tshift_ref, o_ref):
    c = lax.broadcasted_iota(jnp.int32, (N_HEADS, BASE_W), 1)
    ts = tshift_ref[...]
    last = ts[:, 2 * REL_CLIP - 1:2 * REL_CLIP]
    base = jnp.where(c >= 2 * REL_CLIP, last, ts) * LOG2E
    far = base[:, BASE_W - HALF:]

    qchunk = lax.broadcasted_iota(jnp.int32, (N_HEADS, HALF), 1) // CHUNK
    d_far = (lax.broadcasted_iota(jnp.int32, (N_HEADS, _FAR_ROWS, HALF), 1) // CHUNK
             - lax.broadcasted_iota(jnp.int32, (N_HEADS, _FAR_ROWS, HALF), 2) // CHUNK)
    o_ref[:, 0:_FAR_ROWS, :] = jnp.where(d_far >= 0, far[:, None, :], NEG)

    def row(p, carry):
        rolled = pltpu.roll(base, (p + (BASE_W - HALF_WIN + 1)) % BASE_W, 1)[:, 0:HALF]
        val = jnp.where(p // CHUNK - qchunk <= N_PREV, rolled, NEG)
        o_ref[:, pl.ds(p, 1), :] = val[:, None, :]
        return carry

    lax.fori_loop(_FAR_ROWS, HALF_WIN, row, 0, unroll=8)


def _tile_bias(tshift):
    return pl.pallas_call(
        _bias_kernel,
        out_shape=jax.ShapeDtypeStruct((N_HEADS, HALF_WIN, HALF), f32),
        compiler_params=pltpu.CompilerParams(vmem_limit_bytes=VMEM_LIMIT),
        name="rel_bias_tile",
    )(tshift)


def _init_residual(u_scr):
    @pl.when(pl.program_id(0) == 0)
    def _():
        u_scr[...] = jnp.zeros(u_scr.shape, f32)


def _norm_previous_tile(u_scr, g_ref, b_ref, o_ref):
    u = u_scr[...]
    mu = jnp.mean(u, axis=-1, keepdims=True)
    uc = u - mu
    var = jnp.mean(uc * uc, axis=-1, keepdims=True)
    o_ref[...] = uc * lax.rsqrt(var + LN_EPS) * g_ref[...] + b_ref[...]


def _layer_tail(xf, xb, mix, w_qm, w_z, w_out_ref, kvm_ref, u_scr):
    qm = _dot(xb, w_qm) * (LOG2E / math.sqrt(MEM_HEAD_DIM))
    mem_outs = []
    for h in range(MEM_HEADS):
        lo = h * MEM_HEAD_DIM
        qh = qm[:, lo:lo + MEM_HEAD_DIM].astype(bf16)
        kh = kvm_ref[:, lo:lo + MEM_HEAD_DIM]
        vh = kvm_ref[:, E_MEM + lo:E_MEM + lo + MEM_HEAD_DIM]
        mem_outs.append(_softmax2_pv(_dot_nt(qh, kh), vh))
    z = _dot(xb, w_z)
    gate = z * jax.nn.sigmoid(z)
    y = jnp.concatenate([mix] + mem_outs, axis=-1) * gate
    out = _dot(y.astype(bf16), w_out_ref[...])
    u_scr[...] = DN_ALPHA * xf + out


def _attn_layer_kernel(x_ref, w_tok_ref, w_qvt_ref, w_out_ref, kvm_ref, bias_ref, g_ref, b_ref,
                       o_ref, u_scr, qt_scr, k_scr, vt_scr, s_scr, p_scr, mixt_scr, *, n_tiles):
    si = lax.rem(pl.program_id(0), n_tiles)
    _init_residual(u_scr)

    def shift(c):
        k_scr[:, c * TILE:(c + 1) * TILE, :] = k_scr[:, (c + 1) * TILE:(c + 2) * TILE, :]
        vt_scr[:, :, c * TILE:(c + 1) * TILE] = vt_scr[:, :, (c + 1) * TILE:(c + 2) * TILE]

    @pl.when(si == 1)
    def _():
        shift(1)

    @pl.when(si >= 2)
    def _():
        shift(0)
        shift(1)

    _norm_previous_tile(u_scr, g_ref, b_ref, o_ref)
    xf = x_ref[...]
    xb = xf.astype(bf16)
    qt = _dot_nt(w_qvt_ref[0:E_MIX, :], xb) * (LOG2E / math.sqrt(HEAD_DIM))
    qt_scr[...] = qt.astype(bf16).reshape(N_PAIRS, LANES, TILE)
    vt = _dot_nt(w_qvt_ref[E_MIX:2 * E_MIX, :], xb)
    vt_scr[:, ONES_ROWS:ONES_ROWS + LANES, HIST:WIN] = vt.astype(bf16).reshape(N_PAIRS, LANES, TILE)
    ones = jnp.ones((N_PAIRS, ONES_ROWS, TILE), bf16)
    vt_scr[:, 0:ONES_ROWS, HIST:WIN] = ones
    vt_scr[:, ONES_ROWS + LANES:VT_ROWS, HIST:WIN] = ones
    k = _dot(xb, w_tok_ref[:, _AK0:_AK0 + E_MIX])
    for hp in range(N_PAIRS):
        k_scr[hp, HIST:WIN, :] = k[:, hp * LANES:(hp + 1) * LANES].astype(bf16)

    def attn_core(w0):
        zeros = jnp.zeros((HEAD_DIM, TILE), bf16)

        def scores(hp, e):
            lo = e * HEAD_DIM
            qth = qt_scr[hp, lo:lo + HEAD_DIM, :]
            qtm = jnp.concatenate([qth, zeros] if e == 0 else [zeros, qth], axis=0)
            h = HEADS_PER_PAIR * hp + e
            mid = (w0 + WIN) // 2
            ms = [None, None]
            for a, b in ((w0, mid), (mid, WIN)):
                st = _dot(k_scr[hp, a:b, :], qtm)
                for half in range(2):
                    ra, rb = max(a, half * HALF), min(b, half * HALF + HALF_WIN)
                    if rb <= ra:
                        continue
                    cols = slice(half * HALF, (half + 1) * HALF)
                    sh = st[ra - a:rb - a, cols] + bias_ref[h, ra - half * HALF:rb - half * HALF, :]
                    s_scr[e, ra:rb, cols] = sh
                    mh = jnp.max(sh, axis=0, keepdims=True)
                    ms[half] = mh if ms[half] is None else jnp.maximum(ms[half], mh)
            return ms

        def exponentials(e, ms):
            for half in range(2):
                ra, rb = max(w0, half * HALF), half * HALF + HALF_WIN
                cols = slice(half * HALF, (half + 1) * HALF)
                p_scr[e, ra:rb, cols] = jnp.exp2(s_scr[e, ra:rb, cols] - ms[half]).astype(bf16)

        def weighted_values(hp, e):
            lo = e * HEAD_DIM
            r0 = e * (HEAD_DIM + ONES_ROWS)
            ot = _dot(vt_scr[hp, r0:r0 + HEAD_DIM + ONES_ROWS, w0:WIN], p_scr[e, w0:WIN, :])
            if e == 0:
                l, num = ot[0:1, :], ot[ONES_ROWS:, :]
            else:
                l, num = ot[HEAD_DIM:HEAD_DIM + 1, :], ot[0:HEAD_DIM, :]
            mixt_scr[hp, lo:lo + HEAD_DIM, :] = num / l

        def pair_body(hp, m1):
            weighted_values(hp, 0)
            m0_next = scores(hp + 1, 0)
            exponentials(1, m1)
            weighted_values(hp, 1)
            m1_next = scores(hp + 1, 1)
            exponentials(0, m0_next)
            return m1_next

        m0 = scores(0, 0)
        m1 = scores(0, 1)
        exponentials(0, m0)
        for hp in range(N_PAIRS - 1):
            m1 = pair_body(hp, m1)
        weighted_values(N_PAIRS - 1, 0)
        exponentials(1, m1)
        weighted_values(N_PAIRS - 1, 1)

    @pl.when(si == 0)
    def _():
        p_scr[:, HALF_WIN:WIN, 0:HALF] = jnp.zeros((HEADS_PER_PAIR, WIN - HALF_WIN, HALF), bf16)
        p_scr[:, 0:HALF, HALF:TILE] = jnp.zeros((HEADS_PER_PAIR, HALF, HALF), bf16)
        attn_core(HIST)

    @pl.when(si == 1)
    def _():
        attn_core(HIST - TILE)

    @pl.when(si >= 2)
    def _():
        attn_core(0)

    mix = jnp.concatenate([mixt_scr[hp].T for hp in range(N_PAIRS)], axis=-1)
    _layer_tail(xf, xb, mix, w_tok_ref[:, _AQM0:_AQM0 + E_MEM], w_tok_ref[:, _AZ0:_AZ0 + E_BRANCH],
                w_out_ref, kvm_ref, u_scr)


_CARRY = 8


def _conv_layer_kernel(x_ref, w_in_ref, w_out_ref, kvm_ref, cw_ref, g_ref, b_ref, o_ref,
                       u_scr, cu_scr, *, n_tiles):
    si = lax.rem(pl.program_id(0), n_tiles)
    _init_residual(u_scr)

    @pl.when(si == 0)
    def _():
        cu_scr[0:_CARRY, :] = jnp.zeros((_CARRY, E_MIX), f32)

    @pl.when(si > 0)
    def _():
        cu_scr[0:_CARRY, :] = cu_scr[TILE:TILE + _CARRY, :]

    _norm_previous_tile(u_scr, g_ref, b_ref, o_ref)
    xf = x_ref[...]
    xb = xf.astype(bf16)

    p0 = _dot(xb, w_in_ref[:, _Q0:_Q0 + E_MIX])
    p1 = _dot(xb, w_in_ref[:, _K0:_K0 + E_MIX])
    p2 = _dot(xb, w_in_ref[:, _V0:_V0 + E_MIX])
    cu_scr[_CARRY:_CARRY + TILE, :] = p1 * p2
    cw = cw_ref[...]
    conv = cu_scr[_CARRY:_CARRY + TILE, :] * cw[CONV_W - 1:CONV_W, :]
    for t in range(1, CONV_W):
        conv = conv + cu_scr[_CARRY - t:_CARRY - t + TILE, :] * cw[CONV_W - 1 - t:CONV_W - t, :]
    mix = p0 * conv
    _layer_tail(xf, xb, mix, w_in_ref[:, _QM0:_QM0 + E_MEM], w_in_ref[:, _Z0:_Z0 + E_BRANCH],
                w_out_ref, kvm_ref, u_scr)


def _const_spec(shape):
    nd = len(shape)
    return pl.BlockSpec(shape, lambda g: (0,) * nd, pipeline_mode=pl.Buffered(1))


def _layer_call(kind, x, weights, w_out_l, kvm_l, extra, g, b):
    batch, seq, _ = x.shape
    n_tiles = seq // TILE
    last = batch * n_tiles - 1

    def in_tile(g):
        t = jnp.minimum(g, last)
        return t // n_tiles, t % n_tiles

    def out_tile(g):
        t = jnp.maximum(g - 1, 0)
        return t // n_tiles, t % n_tiles

    x_spec = pl.BlockSpec((None, TILE, D_MODEL), lambda g: (*in_tile(g), 0))
    kvm_spec = pl.BlockSpec((None, N_MEM, 2 * E_MEM), lambda g: (in_tile(g)[0], 0, 0))
    w_specs = [_const_spec(w.shape) for w in weights]
    vec_spec = _const_spec((1, D_MODEL))
    tail_specs = [_const_spec((E_BRANCH, D_MODEL)), kvm_spec, _const_spec(extra.shape),
                  vec_spec, vec_spec]
    if kind == "attn":
        body = _attn_layer_kernel
        scratch = [
            pltpu.VMEM((TILE, D_MODEL), f32),
            pltpu.VMEM((N_PAIRS, LANES, TILE), bf16),
            pltpu.VMEM((N_PAIRS, WIN, LANES), bf16),
            pltpu.VMEM((N_PAIRS, VT_ROWS, WIN), bf16),
            pltpu.VMEM((HEADS_PER_PAIR, WIN, TILE), f32),
            pltpu.VMEM((HEADS_PER_PAIR, WIN, TILE), bf16),
            pltpu.VMEM((N_PAIRS, LANES, TILE), f32),
        ]
    else:
        body = _conv_layer_kernel
        scratch = [pltpu.VMEM((TILE, D_MODEL), f32),
                   pltpu.VMEM((TILE + 2 * _CARRY, E_MIX), f32)]
    return pl.pallas_call(
        functools.partial(body, n_tiles=n_tiles),
        grid=(batch * n_tiles + 1,),
        in_specs=[x_spec] + w_specs + tail_specs,
        out_specs=pl.BlockSpec((None, TILE, D_MODEL), lambda g: (*out_tile(g), 0)),
        out_shape=jax.ShapeDtypeStruct(x.shape, f32),
        scratch_shapes=scratch,
        compiler_params=pltpu.CompilerParams(
            dimension_semantics=("arbitrary",), vmem_limit_bytes=VMEM_LIMIT),
        name=f"{kind}_layer",
    )(x, *weights, w_out_l, kvm_l, extra, g, b)


def kernel(x, mem, w_in, w_mem_kv, w_out, rel_bias, conv_w, ln_g, ln_b):
    batch, seq, d = x.shape
    assert d == D_MODEL and seq % TILE == 0 and mem.shape == (batch, N_MEM, D_MODEL)
    w_in_bf = w_in.astype(bf16)
    w_out_bf = w_out.astype(bf16)
    kvm = _kv_mem_all(mem.reshape(batch * N_MEM, D_MODEL), w_mem_kv.astype(bf16))
    kvm = kvm.reshape(DEPTH, batch, N_MEM, 2 * E_MEM)
    tshift = jnp.pad(rel_bias[:, :, 1:], ((0, 0), (0, 0), (0, BASE_W - (N_REL - 1))))
    for layer in range(DEPTH):
        g = ln_g[layer].reshape(1, D_MODEL)
        b = ln_b[layer].reshape(1, D_MODEL)
        w_l = w_in_bf[layer]
        if layer % 2 == 0:
            w_tok = jnp.concatenate([w_l[:, _K0:_K0 + E_MIX], w_l[:, _QM0:]], axis=1)
            w_qvt = jnp.concatenate([w_l[:, _Q0:_Q0 + E_MIX], w_l[:, _V0:_V0 + E_MIX]], axis=1).T
            x = _layer_call("attn", x, [w_tok, w_qvt], w_out_bf[layer], kvm[layer],
                            _tile_bias(tshift[layer // 2]), g, b)
        else:
            x = _layer_call("conv", x, [w_l], w_out_bf[layer], kvm[layer],
                            conv_w[layer // 2], g, b)
    return x
```

```python
import functools
import math

import jax
import jax.numpy as jnp
from jax import lax
from jax.experimental import pallas as pl
from jax.experimental.pallas import tpu as pltpu

D_MODEL = 1024
DEPTH = 4
CHUNK = 64
N_PREV = 8
N_HEADS = 16
HEAD_DIM = 64
E_MIX = N_HEADS * HEAD_DIM
REL_CLIP = 128
N_REL = 2 * REL_CLIP + 1
CONV_W = 3
N_MEM = 256
MEM_HEADS = 4
MEM_HEAD_DIM = 128
E_MEM = MEM_HEADS * MEM_HEAD_DIM
E_BRANCH = E_MIX + E_MEM
N_IN = 3 * E_MIX + E_MEM + E_BRANCH
DN_ALPHA = (2.0 * DEPTH) ** 0.25
LN_EPS = 1e-5

LANES = 128
TILE = 256
STEP_TILES = 2
STEP = STEP_TILES * TILE
HIST = N_PREV * CHUNK
WIN = HIST + TILE
KV_ROWS = HIST + STEP
N_PAIRS = E_MIX // LANES
HEADS_PER_PAIR = LANES // HEAD_DIM
ONES_ROWS = 16
VT_ROWS = LANES + 2 * ONES_ROWS
HALF = TILE // 2
HALF_WIN = HIST + HALF
BASE_W = 384
NEG = -1e30
LOG2E = math.log2(math.e)
VMEM_LIMIT = 56 * 1024 * 1024

_Q0, _K0, _V0 = 0, E_MIX, 2 * E_MIX
_QM0 = 3 * E_MIX
_Z0 = 3 * E_MIX + E_MEM
_AK0, _AQM0, _AZ0 = 0, E_MIX, E_MIX + E_MEM

bf16 = jnp.bfloat16
f32 = jnp.float32


def _dot(a, b):
    return jnp.dot(a, b, preferred_element_type=f32)


def _dot_nt(a, b):
    return lax.dot_general(a, b, (((1,), (1,)), ((), ())), preferred_element_type=f32)


def _softmax2_pv(s, v):
    m = jnp.max(s, axis=-1, keepdims=True)
    p = jnp.exp2(s - m)
    l = jnp.sum(p, axis=-1, keepdims=True)
    return _dot(p.astype(bf16), v) / l


def _kvmem_kernel(mem_ref, w_ref, o_ref):
    o_ref[...] = _dot(mem_ref[...].astype(bf16), w_ref[...]).astype(bf16)


def _kv_mem_all(mem2d, w_mem_kv_bf):
    rows = mem2d.shape[0]
    return pl.pallas_call(
        _kvmem_kernel,
        grid=(DEPTH,),
        in_specs=[
            pl.BlockSpec((rows, D_MODEL), lambda l: (0, 0)),
            pl.BlockSpec((None, D_MODEL, 2 * E_MEM), lambda l: (l, 0, 0)),
        ],
        out_specs=pl.BlockSpec((None, rows, 2 * E_MEM), lambda l: (l, 0, 0)),
        out_shape=jax.ShapeDtypeStruct((DEPTH, rows, 2 * E_MEM), bf16),
        compiler_params=pltpu.CompilerParams(
            dimension_semantics=("arbitrary",), vmem_limit_bytes=VMEM_LIMIT),
        name="kv_mem_proj",
    )(mem2d, w_mem_kv_bf)


_FAR_ROWS = HIST - REL_CLIP


def _bias_kernel(tshift_ref, o_ref):
    c = lax.broadcasted_iota(jnp.int32, (N_HEADS, BASE_W), 1)
    ts = tshift_ref[...]
    last = ts[:, 2 * REL_CLIP - 1:2 * REL_CLIP]
    base = jnp.where(c >= 2 * REL_CLIP, last, ts) * LOG2E
    far = base[:, BASE_W - HALF:]

    qchunk = lax.broadcasted_iota(jnp.int32, (N_HEADS, HALF), 1) // CHUNK
    d_far = (lax.broadcasted_iota(jnp.int32, (N_HEADS, _FAR_ROWS, HALF), 1) // CHUNK
             - lax.broadcasted_iota(jnp.int32, (N_HEADS, _FAR_ROWS, HALF), 2) // CHUNK)
    o_ref[:, 0:_FAR_ROWS, :] = jnp.where(d_far >= 0, far[:, None, :], NEG)

    def row(p, carry):
        rolled = pltpu.roll(base, (p + (BASE_W - HALF_WIN + 1)) % BASE_W, 1)[:, 0:HALF]
        val = jnp.where(p // CHUNK - qchunk <= N_PREV, rolled, NEG)
        o_ref[:, pl.ds(p, 1), :] = val[:, None, :]
        return carry

    lax.fori_loop(_FAR_ROWS, HALF_WIN, row, 0, unroll=8)


def _tile_bias(tshift):
    return pl.pallas_call(
        _bias_kernel,
        out_shape=jax.ShapeDtypeStruct((N_HEADS, HALF_WIN, HALF), f32),
        compiler_params=pltpu.CompilerParams(vmem_limit_bytes=VMEM_LIMIT),
        name="rel_bias_tile",
    )(tshift)


def _init_residual(u_scr):
    @pl.when(pl.program_id(0) == 0)
    def _():
        u_scr[...] = jnp.zeros(u_scr.shape, f32)


def _norm_previous_step(u_scr, g_ref, b_ref, o_ref):
    u = u_scr[...]
    mu = jnp.mean(u, axis=-1, keepdims=True)
    uc = u - mu
    var = jnp.mean(uc * uc, axis=-1, keepdims=True)
    o_ref[...] = uc * lax.rsqrt(var + LN_EPS) * g_ref[...] + b_ref[...]


def _layer_tail(xf, xb, mix, w_qm, w_z, w_out_ref, kvm_ref, u_scr, r0):
    qm = _dot(xb, w_qm) * (LOG2E / math.sqrt(MEM_HEAD_DIM))
    mem_outs = []
    for h in range(MEM_HEADS):
        lo = h * MEM_HEAD_DIM
        qh = qm[:, lo:lo + MEM_HEAD_DIM].astype(bf16)
        kh = kvm_ref[:, lo:lo + MEM_HEAD_DIM]
        vh = kvm_ref[:, E_MEM + lo:E_MEM + lo + MEM_HEAD_DIM]
        mem_outs.append(_softmax2_pv(_dot_nt(qh, kh), vh))
    z = _dot(xb, w_z)
    gate = z * jax.nn.sigmoid(z)
    y = jnp.concatenate([mix] + mem_outs, axis=-1) * gate
    out = _dot(y.astype(bf16), w_out_ref[...])
    u_scr[r0:r0 + TILE, :] = DN_ALPHA * xf + out


def _attn_core(w0, base, qt_scr, k_scr, vt_scr, bias_ref, s_scr, p_scr, mixt_scr):
    zeros = jnp.zeros((HEAD_DIM, TILE), bf16)

    def scores(hp, e):
        lo = e * HEAD_DIM
        qth = qt_scr[hp, lo:lo + HEAD_DIM, :]
        qtm = jnp.concatenate([qth, zeros] if e == 0 else [zeros, qth], axis=0)
        h = HEADS_PER_PAIR * hp + e
        mid = (w0 + WIN) // 2
        ms = [None, None]
        for a, b in ((w0, mid), (mid, WIN)):
            st = _dot(k_scr[hp, base + a:base + b, :], qtm)
            for half in range(2):
                ra, rb = max(a, half * HALF), min(b, half * HALF + HALF_WIN)
                if rb <= ra:
                    continue
                cols = slice(half * HALF, (half + 1) * HALF)
                sh = st[ra - a:rb - a, cols] + bias_ref[h, ra - half * HALF:rb - half * HALF, :]
                s_scr[e, ra:rb, cols] = sh
                mh = jnp.max(sh, axis=0, keepdims=True)
                ms[half] = mh if ms[half] is None else jnp.maximum(ms[half], mh)
        return ms

    def exponentials(e, ms):
        for half in range(2):
            ra, rb = max(w0, half * HALF), half * HALF + HALF_WIN
            cols = slice(half * HALF, (half + 1) * HALF)
            p_scr[e, ra:rb, cols] = jnp.exp2(s_scr[e, ra:rb, cols] - ms[half]).astype(bf16)

    def weighted_values(hp, e):
        lo = e * HEAD_DIM
        r0 = e * (HEAD_DIM + ONES_ROWS)
        ot = _dot(vt_scr[hp, r0:r0 + HEAD_DIM + ONES_ROWS, base + w0:base + WIN],
                  p_scr[e, w0:WIN, :])
        if e == 0:
            l, num = ot[0:1, :], ot[ONES_ROWS:, :]
        else:
            l, num = ot[HEAD_DIM:HEAD_DIM + 1, :], ot[0:HEAD_DIM, :]
        mixt_scr[hp, lo:lo + HEAD_DIM, :] = num / l

    def pair_body(hp, m1):
        weighted_values(hp, 0)
        m0_next = scores(hp + 1, 0)
        exponentials(1, m1)
        weighted_values(hp, 1)
        m1_next = scores(hp + 1, 1)
        exponentials(0, m0_next)
        return m1_next

    m0 = scores(0, 0)
    m1 = scores(0, 1)
    exponentials(0, m0)
    for hp in range(N_PAIRS - 1):
        m1 = pair_body(hp, m1)
    weighted_values(N_PAIRS - 1, 0)
    exponentials(1, m1)
    weighted_values(N_PAIRS - 1, 1)


def _attn_layer_kernel(x_ref, w_tok_ref, w_qvt_ref, w_out_ref, kvm_ref, bias_ref, g_ref, b_ref,
                       o_ref, u_scr, qt_scr, k_scr, vt_scr, s_scr, p_scr, mixt_scr, *, n_steps):
    si = lax.rem(pl.program_id(0), n_steps)
    _init_residual(u_scr)

    @pl.when(si > 0)
    def _():
        k_scr[:, 0:HIST, :] = k_scr[:, STEP:KV_ROWS, :]
        vt_scr[:, :, 0:HIST] = vt_scr[:, :, STEP:KV_ROWS]

    _norm_previous_step(u_scr, g_ref, b_ref, o_ref)
    core = functools.partial(_attn_core, qt_scr=qt_scr, k_scr=k_scr, vt_scr=vt_scr,
                             bias_ref=bias_ref, s_scr=s_scr, p_scr=p_scr, mixt_scr=mixt_scr)
    for sub in range(STEP_TILES):
        r0 = sub * TILE
        xf = x_ref[r0:r0 + TILE, :]
        xb = xf.astype(bf16)
        qt = _dot_nt(w_qvt_ref[0:E_MIX, :], xb) * (LOG2E / math.sqrt(HEAD_DIM))
        qt_scr[...] = qt.astype(bf16).reshape(N_PAIRS, LANES, TILE)
        new = slice(HIST + r0, WIN + r0)
        vt = _dot_nt(w_qvt_ref[E_MIX:2 * E_MIX, :], xb)
        vt_scr[:, ONES_ROWS:ONES_ROWS + LANES, new] = vt.astype(bf16).reshape(N_PAIRS, LANES, TILE)
        ones = jnp.ones((N_PAIRS, ONES_ROWS, TILE), bf16)
        vt_scr[:, 0:ONES_ROWS, new] = ones
        vt_scr[:, ONES_ROWS + LANES:VT_ROWS, new] = ones
        k = _dot(xb, w_tok_ref[:, _AK0:_AK0 + E_MIX])
        for hp in range(N_PAIRS):
            k_scr[hp, new, :] = k[:, hp * LANES:(hp + 1) * LANES].astype(bf16)

        first_w0 = max(HIST - r0, 0)

        @pl.when(si == 0)
        def _():
            if sub == 0:
                p_scr[:, HALF_WIN:WIN, 0:HALF] = jnp.zeros((HEADS_PER_PAIR, WIN - HALF_WIN, HALF), bf16)
                p_scr[:, 0:HALF, HALF:TILE] = jnp.zeros((HEADS_PER_PAIR, HALF, HALF), bf16)
            core(first_w0, r0)

        @pl.when(si > 0)
        def _():
            core(0, r0)

        mix = jnp.concatenate([mixt_scr[hp].T for hp in range(N_PAIRS)], axis=-1)
        _layer_tail(xf, xb, mix, w_tok_ref[:, _AQM0:_AQM0 + E_MEM],
                    w_tok_ref[:, _AZ0:_AZ0 + E_BRANCH], w_out_ref, kvm_ref, u_scr, r0)


_CARRY = 8


def _conv_layer_kernel(x_ref, w_in_ref, w_out_ref, kvm_ref, cw_ref, g_ref, b_ref, o_ref,
                       u_scr, cu_scr, *, n_steps):
    si = lax.rem(pl.program_id(0), n_steps)
    _init_residual(u_scr)

    @pl.when(si == 0)
    def _():
        cu_scr[0:_CARRY, :] = jnp.zeros((_CARRY, E_MIX), f32)

    @pl.when(si > 0)
    def _():
        cu_scr[0:_CARRY, :] = cu_scr[STEP:STEP + _CARRY, :]

    _norm_previous_step(u_scr, g_ref, b_ref, o_ref)
    cw = cw_ref[...]
    for sub in range(STEP_TILES):
        r0 = sub * TILE
        xf = x_ref[r0:r0 + TILE, :]
        xb = xf.astype(bf16)
        p0 = _dot(xb, w_in_ref[:, _Q0:_Q0 + E_MIX])
        p1 = _dot(xb, w_in_ref[:, _K0:_K0 + E_MIX])
        p2 = _dot(xb, w_in_ref[:, _V0:_V0 + E_MIX])
        c0 = _CARRY + r0
        cu_scr[c0:c0 + TILE, :] = p1 * p2
        conv = cu_scr[c0:c0 + TILE, :] * cw[CONV_W - 1:CONV_W, :]
        for t in range(1, CONV_W):
            conv = conv + cu_scr[c0 - t:c0 - t + TILE, :] * cw[CONV_W - 1 - t:CONV_W - t, :]
        mix = p0 * conv
        _layer_tail(xf, xb, mix, w_in_ref[:, _QM0:_QM0 + E_MEM], w_in_ref[:, _Z0:_Z0 + E_BRANCH],
                    w_out_ref, kvm_ref, u_scr, r0)


def _const_spec(shape):
    nd = len(shape)
    return pl.BlockSpec(shape, lambda g: (0,) * nd, pipeline_mode=pl.Buffered(1))


def _layer_call(kind, x, weights, w_out_l, kvm_l, extra, g, b):
    batch, seq, _ = x.shape
    n_steps = seq // STEP
    last = batch * n_steps - 1

    def in_block(g):
        t = jnp.minimum(g, last)
        return t // n_steps, t % n_steps

    def out_block(g):
        t = jnp.maximum(g - 1, 0)
        return t // n_steps, t % n_steps

    x_spec = pl.BlockSpec((None, STEP, D_MODEL), lambda g: (*in_block(g), 0))
    kvm_spec = pl.BlockSpec((None, N_MEM, 2 * E_MEM), lambda g: (in_block(g)[0], 0, 0))
    w_specs = [_const_spec(w.shape) for w in weights]
    vec_spec = _const_spec((1, D_MODEL))
    tail_specs = [_const_spec((E_BRANCH, D_MODEL)), kvm_spec, _const_spec(extra.shape),
                  vec_spec, vec_spec]
    if kind == "attn":
        body = _attn_layer_kernel
        scratch = [
            pltpu.VMEM((STEP, D_MODEL), f32),
            pltpu.VMEM((N_PAIRS, LANES, TILE), bf16),
            pltpu.VMEM((N_PAIRS, KV_ROWS, LANES), bf16),
            pltpu.VMEM((N_PAIRS, VT_ROWS, KV_ROWS), bf16),
            pltpu.VMEM((HEADS_PER_PAIR, WIN, TILE), f32),
            pltpu.VMEM((HEADS_PER_PAIR, WIN, TILE), bf16),
            pltpu.VMEM((N_PAIRS, LANES, TILE), f32),
        ]
    else:
        body = _conv_layer_kernel
        scratch = [pltpu.VMEM((STEP, D_MODEL), f32),
                   pltpu.VMEM((STEP + _CARRY, E_MIX), f32)]
    return pl.pallas_call(
        functools.partial(body, n_steps=n_steps),
        grid=(batch * n_steps + 1,),
        in_specs=[x_spec] + w_specs + tail_specs,
        out_specs=pl.BlockSpec((None, STEP, D_MODEL), lambda g: (*out_block(g), 0)),
        out_shape=jax.ShapeDtypeStruct(x.shape, f32),
        scratch_shapes=scratch,
        compiler_params=pltpu.CompilerParams(
            dimension_semantics=("arbitrary",), vmem_limit_bytes=VMEM_LIMIT),
        name=f"{kind}_layer",
    )(x, *weights, w_out_l, kvm_l, extra, g, b)


def kernel(x, mem, w_in, w_mem_kv, w_out, rel_bias, conv_w, ln_g, ln_b):
    batch, seq, d = x.shape
    assert d == D_MODEL and seq % STEP == 0 and STEP >= HIST
    assert mem.shape == (batch, N_MEM, D_MODEL)
    w_in_bf = w_in.astype(bf16)
    w_out_bf = w_out.astype(bf16)
    kvm = _kv_mem_all(mem.reshape(batch * N_MEM, D_MODEL), w_mem_kv.astype(bf16))
    kvm = kvm.reshape(DEPTH, batch, N_MEM, 2 * E_MEM)
    tshift = jnp.pad(rel_bias[:, :, 1:], ((0, 0), (0, 0), (0, BASE_W - (N_REL - 1))))
    for layer in range(DEPTH):
        g = ln_g[layer].reshape(1, D_MODEL)
        b = ln_b[layer].reshape(1, D_MODEL)
        w_l = w_in_bf[layer]
        if layer % 2 == 0:
            w_tok = jnp.concatenate([w_l[:, _K0:_K0 + E_MIX], w_l[:, _QM0:]], axis=1)
            w_qvt = jnp.concatenate([w_l[:, _Q0:_Q0 + E_MIX], w_l[:, _V0:_V0 + E_MIX]], axis=1).T
            x = _layer_call("attn", x, [w_tok, w_qvt], w_out_bf[layer], kvm[layer],
                            _tile_bias(tshift[layer // 2]), g, b)
        else:
            x = _layer_call("conv", x, [w_l], w_out_bf[layer], kvm[layer],
                            conv_w[layer // 2], g, b)
    return x
```

```python
import functools
import math

import jax
import jax.numpy as jnp
from jax import lax
from jax.experimental import pallas as pl
from jax.experimental.pallas import tpu as pltpu

D_MODEL = 1024
DEPTH = 4
CHUNK = 64
N_PREV = 8
N_HEADS = 16
HEAD_DIM = 64
E_MIX = N_HEADS * HEAD_DIM
REL_CLIP = 128
N_REL = 2 * REL_CLIP + 1
CONV_W = 3
N_MEM = 256
MEM_HEADS = 4
MEM_HEAD_DIM = 128
E_MEM = MEM_HEADS * MEM_HEAD_DIM
E_BRANCH = E_MIX + E_MEM
N_IN = 3 * E_MIX + E_MEM + E_BRANCH
DN_ALPHA = (2.0 * DEPTH) ** 0.25
LN_EPS = 1e-5

LANES = 128
TILE = 256
STEP_TILES = 2
STEP = STEP_TILES * TILE
HIST = N_PREV * CHUNK
WIN = HIST + TILE
KV_ROWS = HIST + STEP
N_PAIRS = E_MIX // LANES
HEADS_PER_PAIR = LANES // HEAD_DIM
ONES_ROWS = 16
VT_ROWS = LANES + 2 * ONES_ROWS
QK_ROWS = 128
K_LANES = 2 * LANES
HALF = TILE // 2
HALF_WIN = HIST + HALF
BASE_W = 384
NEG = -1e30
LOG2E = math.log2(math.e)
VMEM_LIMIT = 56 * 1024 * 1024

_Q0, _K0, _V0 = 0, E_MIX, 2 * E_MIX
_QM0 = 3 * E_MIX
_Z0 = 3 * E_MIX + E_MEM
_AK0, _AQM0, _AZ0 = 0, E_MIX, E_MIX + E_MEM

bf16 = jnp.bfloat16
f32 = jnp.float32


def _dot(a, b):
    return jnp.dot(a, b, preferred_element_type=f32)


def _dot_nt(a, b):
    return lax.dot_general(a, b, (((1,), (1,)), ((), ())), preferred_element_type=f32)


def _softmax2_pv(s, v):
    m = jnp.max(s, axis=-1, keepdims=True)
    p = jnp.exp2(s - m)
    l = jnp.sum(p, axis=-1, keepdims=True)
    return _dot(p.astype(bf16), v) / l


def _kvmem_kernel(mem_ref, w_ref, o_ref):
    o_ref[...] = _dot(mem_ref[...].astype(bf16), w_ref[...]).astype(bf16)


def _kv_mem_all(mem2d, w_mem_kv_bf):
    rows = mem2d.shape[0]
    return pl.pallas_call(
        _kvmem_kernel,
        grid=(DEPTH,),
        in_specs=[
            pl.BlockSpec((rows, D_MODEL), lambda l: (0, 0)),
            pl.BlockSpec((None, D_MODEL, 2 * E_MEM), lambda l: (l, 0, 0)),
        ],
        out_specs=pl.BlockSpec((None, rows, 2 * E_MEM), lambda l: (l, 0, 0)),
        out_shape=jax.ShapeDtypeStruct((DEPTH, rows, 2 * E_MEM), bf16),
        compiler_params=pltpu.CompilerParams(
            dimension_semantics=("arbitrary",), vmem_limit_bytes=VMEM_LIMIT),
        name="kv_mem_proj",
    )(mem2d, w_mem_kv_bf)


_FAR_ROWS = HIST - REL_CLIP


def _bias_kernel(tshift_ref, o_ref):
    c = lax.broadcasted_iota(jnp.int32, (N_HEADS, BASE_W), 1)
    ts = tshift_ref[...]
    last = ts[:, 2 * REL_CLIP - 1:2 * REL_CLIP]
    base = jnp.where(c >= 2 * REL_CLIP, last, ts) * LOG2E
    far = base[:, BASE_W - HALF:]

    qchunk = lax.broadcasted_iota(jnp.int32, (N_HEADS, HALF), 1) // CHUNK
    d_far = (lax.broadcasted_iota(jnp.int32, (N_HEADS, _FAR_ROWS, HALF), 1) // CHUNK
             - lax.broadcasted_iota(jnp.int32, (N_HEADS, _FAR_ROWS, HALF), 2) // CHUNK)
    o_ref[:, 0:_FAR_ROWS, :] = jnp.where(d_far >= 0, far[:, None, :], NEG)

    def row(p, carry):
        rolled = pltpu.roll(base, (p + (BASE_W - HALF_WIN + 1)) % BASE_W, 1)[:, 0:HALF]
        val = jnp.where(p // CHUNK - qchunk <= N_PREV, rolled, NEG)
        o_ref[:, pl.ds(p, 1), :] = val[:, None, :]
        return carry

    lax.fori_loop(_FAR_ROWS, HALF_WIN, row, 0, unroll=8)


def _tile_bias(tshift):
    return pl.pallas_call(
        _bias_kernel,
        out_shape=jax.ShapeDtypeStruct((N_HEADS, HALF_WIN, HALF), f32),
        compiler_params=pltpu.CompilerParams(vmem_limit_bytes=VMEM_LIMIT),
        name="rel_bias_tile",
    )(tshift)


def _init_residual(u_scr):
    @pl.when(pl.program_id(0) == 0)
    def _():
        u_scr[...] = jnp.zeros(u_scr.shape, f32)


def _norm_previous_step(u_scr, g_ref, b_ref, o_ref):
    u = u_scr[...]
    mu = jnp.mean(u, axis=-1, keepdims=True)
    uc = u - mu
    var = jnp.mean(uc * uc, axis=-1, keepdims=True)
    o_ref[...] = uc * lax.rsqrt(var + LN_EPS) * g_ref[...] + b_ref[...]


def _layer_tail(xf, xb, mix, w_qm, w_z, w_out_ref, kvm_ref, u_scr, r0):
    qm = _dot(xb, w_qm) * (LOG2E / math.sqrt(MEM_HEAD_DIM))
    mem_outs = []
    for h in range(MEM_HEADS):
        lo = h * MEM_HEAD_DIM
        qh = qm[:, lo:lo + MEM_HEAD_DIM].astype(bf16)
        kh = kvm_ref[:, lo:lo + MEM_HEAD_DIM]
        vh = kvm_ref[:, E_MEM + lo:E_MEM + lo + MEM_HEAD_DIM]
        mem_outs.append(_softmax2_pv(_dot_nt(qh, kh), vh))
    z = _dot(xb, w_z)
    gate = z * jax.nn.sigmoid(z)
    y = jnp.concatenate([mix] + mem_outs, axis=-1) * gate
    out = _dot(y.astype(bf16), w_out_ref[...])
    u_scr[r0:r0 + TILE, :] = DN_ALPHA * xf + out


def _attn_core(base, qt_scr, k_scr, vt_scr, bias_ref, s_scr, p_scr, mixt_scr):
    w0 = 0
    zeros = jnp.zeros((HEAD_DIM, TILE), bf16)
    flag_rows = jnp.where(lax.broadcasted_iota(jnp.int32, (LANES, TILE), 0) == 0, NEG, 0.0).astype(bf16)

    def scores(hp, e):
        lo = e * HEAD_DIM
        qth = qt_scr[hp, lo:lo + HEAD_DIM, :]
        qtm = jnp.concatenate([qth, zeros, flag_rows] if e == 0 else [zeros, qth, flag_rows], axis=0)
        h = HEADS_PER_PAIR * hp + e
        ms = [None, None]
        for a, b in [(r, r + QK_ROWS) for r in range(w0, WIN, QK_ROWS)]:
            halves = [half for half in range(2)
                      if max(a, half * HALF) < min(b, half * HALF + HALF_WIN)]
            c0 = halves[0] * HALF
            st = _dot(k_scr[hp, base + a:base + b, :], qtm[:, c0:(halves[-1] + 1) * HALF])
            for half in halves:
                ra, rb = max(a, half * HALF), min(b, half * HALF + HALF_WIN)
                cols = slice(half * HALF, (half + 1) * HALF)
                sh = (st[ra - a:rb - a, half * HALF - c0:(half + 1) * HALF - c0]
                      + bias_ref[h, ra - half * HALF:rb - half * HALF, :])
                s_scr[e, ra:rb, cols] = sh
                mh = jnp.max(sh, axis=0, keepdims=True)
                ms[half] = mh if ms[half] is None else jnp.maximum(ms[half], mh)
        return ms

    def exponentials(e, ms):
        for half in range(2):
            ra, rb = max(w0, half * HALF), half * HALF + HALF_WIN
            cols = slice(half * HALF, (half + 1) * HALF)
            p_scr[e, ra:rb, cols] = jnp.exp2(s_scr[e, ra:rb, cols] - ms[half]).astype(bf16)

    def weighted_values(hp, e):
        lo = e * HEAD_DIM
        r0 = e * (HEAD_DIM + ONES_ROWS)
        ot = _dot(vt_scr[hp, r0:r0 + HEAD_DIM + ONES_ROWS, base + w0:base + WIN],
                  p_scr[e, w0:WIN, :])
        if e == 0:
            l, num = ot[0:1, :], ot[ONES_ROWS:, :]
        else:
            l, num = ot[HEAD_DIM:HEAD_DIM + 1, :], ot[0:HEAD_DIM, :]
        mixt_scr[hp, lo:lo + HEAD_DIM, :] = num / l

    def pair_body(hp, m1):
        weighted_values(hp, 0)
        m0_next = scores(hp + 1, 0)
        exponentials(1, m1)
        weighted_values(hp, 1)
        m1_next = scores(hp + 1, 1)
        exponentials(0, m0_next)
        return m1_next

    m0 = scores(0, 0)
    m1 = scores(0, 1)
    exponentials(0, m0)
    for hp in range(N_PAIRS - 1):
        m1 = pair_body(hp, m1)
    weighted_values(N_PAIRS - 1, 0)
    exponentials(1, m1)
    weighted_values(N_PAIRS - 1, 1)


def _attn_layer_kernel(x_ref, w_tok_ref, w_qvt_ref, w_out_ref, kvm_ref, bias_ref, g_ref, b_ref,
                       o_ref, u_scr, qt_scr, k_scr, vt_scr, s_scr, p_scr, mixt_scr, *, n_steps):
    si = lax.rem(pl.program_id(0), n_steps)
    _init_residual(u_scr)

    @pl.when(pl.program_id(0) == 0)
    def _():
        k_scr[:, :, LANES:K_LANES] = jnp.zeros((N_PAIRS, KV_ROWS, LANES), bf16)
        p_scr[:, HALF_WIN:WIN, 0:HALF] = jnp.zeros((HEADS_PER_PAIR, WIN - HALF_WIN, HALF), bf16)
        p_scr[:, 0:HALF, HALF:TILE] = jnp.zeros((HEADS_PER_PAIR, HALF, HALF), bf16)

    @pl.when(si == 0)
    def _():
        lane = lax.broadcasted_iota(jnp.int32, (N_PAIRS, HIST, K_LANES), 2)
        k_scr[:, 0:HIST, :] = jnp.where(lane == LANES, 1.0, 0.0).astype(bf16)
        vt_scr[:, :, 0:HIST] = jnp.zeros((N_PAIRS, VT_ROWS, HIST), bf16)

    @pl.when(si > 0)
    def _():
        k_scr[:, 0:HIST, :] = k_scr[:, STEP:KV_ROWS, :]
        vt_scr[:, :, 0:HIST] = vt_scr[:, :, STEP:KV_ROWS]

    _norm_previous_step(u_scr, g_ref, b_ref, o_ref)
    core = functools.partial(_attn_core, qt_scr=qt_scr, k_scr=k_scr, vt_scr=vt_scr,
                             bias_ref=bias_ref, s_scr=s_scr, p_scr=p_scr, mixt_scr=mixt_scr)
    for sub in range(STEP_TILES):
        r0 = sub * TILE
        xf = x_ref[r0:r0 + TILE, :]
        xb = xf.astype(bf16)
        qt = _dot_nt(w_qvt_ref[0:E_MIX, :], xb) * (LOG2E / math.sqrt(HEAD_DIM))
        qt_scr[...] = qt.astype(bf16).reshape(N_PAIRS, LANES, TILE)
        new = slice(HIST + r0, WIN + r0)
        vt = _dot_nt(w_qvt_ref[E_MIX:2 * E_MIX, :], xb)
        vt_scr[:, ONES_ROWS:ONES_ROWS + LANES, new] = vt.astype(bf16).reshape(N_PAIRS, LANES, TILE)
        ones = jnp.ones((N_PAIRS, ONES_ROWS, TILE), bf16)
        vt_scr[:, 0:ONES_ROWS, new] = ones
        vt_scr[:, ONES_ROWS + LANES:VT_ROWS, new] = ones
        k = _dot(xb, w_tok_ref[:, _AK0:_AK0 + E_MIX])
        for hp in range(N_PAIRS):
            k_scr[hp, new, 0:LANES] = k[:, hp * LANES:(hp + 1) * LANES].astype(bf16)

        core(r0)
        mix = jnp.concatenate([mixt_scr[hp].T for hp in range(N_PAIRS)], axis=-1)
        _layer_tail(xf, xb, mix, w_tok_ref[:, _AQM0:_AQM0 + E_MEM],
                    w_tok_ref[:, _AZ0:_AZ0 + E_BRANCH], w_out_ref, kvm_ref, u_scr, r0)


_CARRY = 8


def _conv_layer_kernel(x_ref, w_in_ref, w_out_ref, kvm_ref, cw_ref, g_ref, b_ref, o_ref,
                       u_scr, cu_scr, *, n_steps):
    si = lax.rem(pl.program_id(0), n_steps)
    _init_residual(u_scr)

    @pl.when(si == 0)
    def _():
        cu_scr[0:_CARRY, :] = jnp.zeros((_CARRY, E_MIX), f32)

    @pl.when(si > 0)
    def _():
        cu_scr[0:_CARRY, :] = cu_scr[STEP:STEP + _CARRY, :]

    _norm_previous_step(u_scr, g_ref, b_ref, o_ref)
    cw = cw_ref[...]
    for sub in range(STEP_TILES):
        r0 = sub * TILE
        xf = x_ref[r0:r0 + TILE, :]
        xb = xf.astype(bf16)
        p0 = _dot(xb, w_in_ref[:, _Q0:_Q0 + E_MIX])
        p1 = _dot(xb, w_in_ref[:, _K0:_K0 + E_MIX])
        p2 = _dot(xb, w_in_ref[:, _V0:_V0 + E_MIX])
        c0 = _CARRY + r0
        cu_scr[c0:c0 + TILE, :] = p1 * p2
        conv = cu_scr[c0:c0 + TILE, :] * cw[CONV_W - 1:CONV_W, :]
        for t in range(1, CONV_W):
            conv = conv + cu_scr[c0 - t:c0 - t + TILE, :] * cw[CONV_W - 1 - t:CONV_W - t, :]
        mix = p0 * conv
        _layer_tail(xf, xb, mix, w_in_ref[:, _QM0:_QM0 + E_MEM], w_in_ref[:, _Z0:_Z0 + E_BRANCH],
                    w_out_ref, kvm_ref, u_scr, r0)


def _const_spec(shape):
    nd = len(shape)
    return pl.BlockSpec(shape, lambda g: (0,) * nd, pipeline_mode=pl.Buffered(1))


def _layer_call(kind, x, weights, w_out_l, kvm_l, extra, g, b):
    batch, seq, _ = x.shape
    n_steps = seq // STEP
    last = batch * n_steps - 1

    def in_block(g):
        t = jnp.minimum(g, last)
        return t // n_steps, t % n_steps

    def out_block(g):
        t = jnp.maximum(g - 1, 0)
        return t // n_steps, t % n_steps

    x_spec = pl.BlockSpec((None, STEP, D_MODEL), lambda g: (*in_block(g), 0))
    kvm_spec = pl.BlockSpec((None, N_MEM, 2 * E_MEM), lambda g: (in_block(g)[0], 0, 0))
    w_specs = [_const_spec(w.shape) for w in weights]
    vec_spec = _const_spec((1, D_MODEL))
    tail_specs = [_const_spec((E_BRANCH, D_MODEL)), kvm_spec, _const_spec(extra.shape),
                  vec_spec, vec_spec]
    if kind == "attn":
        body = _attn_layer_kernel
        scratch = [
            pltpu.VMEM((STEP, D_MODEL), f32),
            pltpu.VMEM((N_PAIRS, LANES, TILE), bf16),
            pltpu.VMEM((N_PAIRS, KV_ROWS, K_LANES), bf16),
            pltpu.VMEM((N_PAIRS, VT_ROWS, KV_ROWS), bf16),
            pltpu.VMEM((HEADS_PER_PAIR, WIN, TILE), f32),
            pltpu.VMEM((HEADS_PER_PAIR, WIN, TILE), bf16),
            pltpu.VMEM((N_PAIRS, LANES, TILE), f32),
        ]
    else:
        body = _conv_layer_kernel
        scratch = [pltpu.VMEM((STEP, D_MODEL), f32),
                   pltpu.VMEM((STEP + _CARRY, E_MIX), f32)]
    return pl.pallas_call(
        functools.partial(body, n_steps=n_steps),
        grid=(batch * n_steps + 1,),
        in_specs=[x_spec] + w_specs + tail_specs,
        out_specs=pl.BlockSpec((None, STEP, D_MODEL), lambda g: (*out_block(g), 0)),
        out_shape=jax.ShapeDtypeStruct(x.shape, f32),
        scratch_shapes=scratch,
        compiler_params=pltpu.CompilerParams(
            dimension_semantics=("arbitrary",), vmem_limit_bytes=VMEM_LIMIT),
        name=f"{kind}_layer",
    )(x, *weights, w_out_l, kvm_l, extra, g, b)


def kernel(x, mem, w_in, w_mem_kv, w_out, rel_bias, conv_w, ln_g, ln_b):
    batch, seq, d = x.shape
    assert d == D_MODEL and seq % STEP == 0 and STEP >= HIST
    assert mem.shape == (batch, N_MEM, D_MODEL)
    w_in_bf = w_in.astype(bf16)
    w_out_bf = w_out.astype(bf16)
    kvm = _kv_mem_all(mem.reshape(batch * N_MEM, D_MODEL), w_mem_kv.astype(bf16))
    kvm = kvm.reshape(DEPTH, batch, N_MEM, 2 * E_MEM)
    tshift = jnp.pad(rel_bias[:, :, 1:], ((0, 0), (0, 0), (0, BASE_W - (N_REL - 1))))
    for layer in range(DEPTH):
        g = ln_g[layer].reshape(1, D_MODEL)
        b = ln_b[layer].reshape(1, D_MODEL)
        w_l = w_in_bf[layer]
        if layer % 2 == 0:
            w_tok = jnp.concatenate([w_l[:, _K0:_K0 + E_MIX], w_l[:, _QM0:]], axis=1)
            w_qvt = jnp.concatenate([w_l[:, _Q0:_Q0 + E_MIX], w_l[:, _V0:_V0 + E_MIX]], axis=1).T
            x = _layer_call("attn", x, [w_tok, w_qvt], w_out_bf[layer], kvm[layer],
                            _tile_bias(tshift[layer // 2]), g, b)
        else:
            x = _layer_call("conv", x, [w_l], w_out_bf[layer], kvm[layer],
                            conv_w[layer // 2], g, b)
    return x
```

```python
import functools
import math

import jax
import jax.numpy as jnp
from jax import lax
from jax.experimental import pallas as pl
from jax.experimental.pallas import tpu as pltpu

D_MODEL = 1024
DEPTH = 4
CHUNK = 64
N_PREV = 8
N_HEADS = 16
HEAD_DIM = 64
E_MIX = N_HEADS * HEAD_DIM
REL_CLIP = 128
N_REL = 2 * REL_CLIP + 1
CONV_W = 3
N_MEM = 256
MEM_HEADS = 4
MEM_HEAD_DIM = 128
E_MEM = MEM_HEADS * MEM_HEAD_DIM
E_BRANCH = E_MIX + E_MEM
N_IN = 3 * E_MIX + E_MEM + E_BRANCH
DN_ALPHA = (2.0 * DEPTH) ** 0.25
LN_EPS = 1e-5

LANES = 128
TILE = 256
STEP_TILES = 2
STEP = STEP_TILES * TILE
HIST = N_PREV * CHUNK
WIN = HIST + TILE
KV_ROWS = HIST + STEP
N_PAIRS = E_MIX // LANES
HEADS_PER_PAIR = LANES // HEAD_DIM
ONES_ROWS = 16
VT_ROWS = LANES + 2 * ONES_ROWS
QK_ROWS = 128
K_LANES = 2 * LANES
HALF = TILE // 2
HALF_WIN = HIST + HALF
BASE_W = 384
NEG = -1e30
LOG2E = math.log2(math.e)
VMEM_LIMIT = 56 * 1024 * 1024

_Q0, _K0, _V0 = 0, E_MIX, 2 * E_MIX
_QM0 = 3 * E_MIX
_Z0 = 3 * E_MIX + E_MEM
_AK0, _AQM0, _AZ0 = 0, E_MIX, E_MIX + E_MEM

bf16 = jnp.bfloat16
f32 = jnp.float32


def _dot(a, b):
    return jnp.dot(a, b, preferred_element_type=f32)


def _dot_nt(a, b):
    return lax.dot_general(a, b, (((1,), (1,)), ((), ())), preferred_element_type=f32)


def _softmax2_pv(s, v):
    m = jnp.max(s, axis=-1, keepdims=True)
    p = jnp.exp2(s - m)
    l = jnp.sum(p, axis=-1, keepdims=True)
    return _dot(p.astype(bf16), v) / l


def _kvmem_kernel(mem_ref, w_ref, o_ref):
    o_ref[...] = _dot(mem_ref[...].astype(bf16), w_ref[...]).astype(bf16)


def _kv_mem_all(mem2d, w_mem_kv_bf):
    rows = mem2d.shape[0]
    return pl.pallas_call(
        _kvmem_kernel,
        grid=(DEPTH,),
        in_specs=[
            pl.BlockSpec((rows, D_MODEL), lambda l: (0, 0)),
            pl.BlockSpec((None, D_MODEL, 2 * E_MEM), lambda l: (l, 0, 0)),
        ],
        out_specs=pl.BlockSpec((None, rows, 2 * E_MEM), lambda l: (l, 0, 0)),
        out_shape=jax.ShapeDtypeStruct((DEPTH, rows, 2 * E_MEM), bf16),
        compiler_params=pltpu.CompilerParams(
            dimension_semantics=("arbitrary",), vmem_limit_bytes=VMEM_LIMIT),
        name="kv_mem_proj",
    )(mem2d, w_mem_kv_bf)


_FAR_ROWS = HIST - REL_CLIP


def _bias_kernel(tshift_ref, o_ref):
    c = lax.broadcasted_iota(jnp.int32, (N_HEADS, BASE_W), 1)
    ts = tshift_ref[...]
    last = ts[:, 2 * REL_CLIP - 1:2 * REL_CLIP]
    base = jnp.where(c >= 2 * REL_CLIP, last, ts) * LOG2E
    far = base[:, BASE_W - HALF:]

    qchunk = lax.broadcasted_iota(jnp.int32, (N_HEADS, HALF), 1) // CHUNK
    d_far = (lax.broadcasted_iota(jnp.int32, (N_HEADS, _FAR_ROWS, HALF), 1) // CHUNK
             - lax.broadcasted_iota(jnp.int32, (N_HEADS, _FAR_ROWS, HALF), 2) // CHUNK)
    o_ref[:, 0:_FAR_ROWS, :] = jnp.where(d_far >= 0, far[:, None, :], NEG)

    def row(p, carry):
        rolled = pltpu.roll(base, (p + (BASE_W - HALF_WIN + 1)) % BASE_W, 1)[:, 0:HALF]
        val = jnp.where(p // CHUNK - qchunk <= N_PREV, rolled, NEG)
        o_ref[:, pl.ds(p, 1), :] = val[:, None, :]
        return carry

    lax.fori_loop(_FAR_ROWS, HALF_WIN, row, 0, unroll=8)


def _tile_bias(tshift):
    return pl.pallas_call(
        _bias_kernel,
        out_shape=jax.ShapeDtypeStruct((N_HEADS, HALF_WIN, HALF), f32),
        compiler_params=pltpu.CompilerParams(vmem_limit_bytes=VMEM_LIMIT),
        name="rel_bias_tile",
    )(tshift)


def _init_residual(u_scr):
    @pl.when(pl.program_id(0) == 0)
    def _():
        u_scr[...] = jnp.zeros(u_scr.shape, f32)


def _norm_previous_step(u_scr, g_ref, b_ref, o_ref):
    u = u_scr[...]
    mu = jnp.mean(u, axis=-1, keepdims=True)
    uc = u - mu
    var = jnp.mean(uc * uc, axis=-1, keepdims=True)
    o_ref[...] = uc * lax.rsqrt(var + LN_EPS) * g_ref[...] + b_ref[...]


def _layer_tail(xf, xb, mix, w_qm, w_z, w_out_ref, kvm_ref, u_scr, r0):
    qm = _dot(xb, w_qm) * (LOG2E / math.sqrt(MEM_HEAD_DIM))
    mem_outs = []
    for h in range(MEM_HEADS):
        lo = h * MEM_HEAD_DIM
        qh = qm[:, lo:lo + MEM_HEAD_DIM].astype(bf16)
        kh = kvm_ref[:, lo:lo + MEM_HEAD_DIM]
        vh = kvm_ref[:, E_MEM + lo:E_MEM + lo + MEM_HEAD_DIM]
        mem_outs.append(_softmax2_pv(_dot_nt(qh, kh), vh))
    z = _dot(xb, w_z)
    gate = z * jax.nn.sigmoid(z)
    y = jnp.concatenate([mix] + mem_outs, axis=-1) * gate
    out = _dot(y.astype(bf16), w_out_ref[...])
    u_scr[r0:r0 + TILE, :] = DN_ALPHA * xf + out


def _attn_core(base, qt_scr, k_scr, vt_scr, bias_ref, s_scr, p_scr, mixt_scr):
    zeros = jnp.zeros((HEAD_DIM, TILE), bf16)
    flag_rows = jnp.where(lax.broadcasted_iota(jnp.int32, (LANES, TILE), 0) == 0, NEG, 0.0).astype(bf16)

    def scores(hp, e):
        lo = e * HEAD_DIM
        qth = qt_scr[hp, lo:lo + HEAD_DIM, :]
        qtm = jnp.concatenate([qth, zeros, flag_rows] if e == 0 else [zeros, qth, flag_rows], axis=0)
        h = HEADS_PER_PAIR * hp + e
        ms = [None, None]
        for a, b in [(r, r + QK_ROWS) for r in range(0, WIN, QK_ROWS)]:
            halves = [half for half in range(2)
                      if max(a, half * HALF) < min(b, half * HALF + HALF_WIN)]
            c0 = halves[0] * HALF
            st = _dot(k_scr[hp, base + a:base + b, :], qtm[:, c0:(halves[-1] + 1) * HALF])
            for half in halves:
                ra, rb = max(a, half * HALF), min(b, half * HALF + HALF_WIN)
                cols = slice(half * HALF, (half + 1) * HALF)
                sh = (st[ra - a:rb - a, half * HALF - c0:(half + 1) * HALF - c0]
                      + bias_ref[h, ra - half * HALF:rb - half * HALF, :])
                s_scr[e, ra:rb, cols] = sh
                mh = jnp.max(sh, axis=0, keepdims=True)
                ms[half] = mh if ms[half] is None else jnp.maximum(ms[half], mh)
        return ms

    def exponentials(e, ms):
        for half in range(2):
            ra, rb = half * HALF, half * HALF + HALF_WIN
            cols = slice(half * HALF, (half + 1) * HALF)
            p_scr[e, ra:rb, cols] = jnp.exp2(s_scr[e, ra:rb, cols] - ms[half]).astype(bf16)

    def weighted_values(hp, e):
        lo = e * HEAD_DIM
        r0 = e * (HEAD_DIM + ONES_ROWS)
        ot = _dot(vt_scr[hp, r0:r0 + HEAD_DIM + ONES_ROWS, base:base + WIN], p_scr[e])
        if e == 0:
            l, num = ot[0:1, :], ot[ONES_ROWS:, :]
        else:
            l, num = ot[HEAD_DIM:HEAD_DIM + 1, :], ot[0:HEAD_DIM, :]
        mixt_scr[hp, lo:lo + HEAD_DIM, :] = num / l

    def pair_body(hp, m1):
        weighted_values(hp, 0)
        m0_next = scores(hp + 1, 0)
        exponentials(1, m1)
        weighted_values(hp, 1)
        m1_next = scores(hp + 1, 1)
        exponentials(0, m0_next)
        return m1_next

    m0 = scores(0, 0)
    m1 = scores(0, 1)
    exponentials(0, m0)
    for hp in range(N_PAIRS - 1):
        m1 = pair_body(hp, m1)
    weighted_values(N_PAIRS - 1, 0)
    exponentials(1, m1)
    weighted_values(N_PAIRS - 1, 1)


def _attn_layer_kernel(x_ref, w_tok_ref, w_qvt_ref, w_out_ref, kvm_ref, bias_ref, g_ref, b_ref,
                       o_ref, u_scr, qt_scr, k_scr, vt_scr, s_scr, p_scr, mixt_scr, *, n_steps):
    si = lax.rem(pl.program_id(0), n_steps)
    _init_residual(u_scr)

    @pl.when(pl.program_id(0) == 0)
    def _():
        k_scr[...] = jnp.zeros(k_scr.shape, bf16)
        vt_scr[...] = jnp.zeros(vt_scr.shape, bf16)
        p_scr[:, HALF_WIN:WIN, 0:HALF] = jnp.zeros((HEADS_PER_PAIR, WIN - HALF_WIN, HALF), bf16)
        p_scr[:, 0:HALF, HALF:TILE] = jnp.zeros((HEADS_PER_PAIR, HALF, HALF), bf16)

    lane = lax.broadcasted_iota(jnp.int32, (HIST, K_LANES), 1)
    no_history = jnp.where(lane == LANES, 1.0, 0.0).astype(bf16)
    for hp in range(N_PAIRS):
        k_scr[hp, 0:HIST, :] = jnp.where(si == 0, no_history, k_scr[hp, STEP:KV_ROWS, :])
        vt_scr[hp, :, 0:HIST] = jnp.where(si == 0, jnp.zeros((VT_ROWS, HIST), bf16),
                                          vt_scr[hp, :, STEP:KV_ROWS])

    _norm_previous_step(u_scr, g_ref, b_ref, o_ref)
    core = functools.partial(_attn_core, qt_scr=qt_scr, k_scr=k_scr, vt_scr=vt_scr,
                             bias_ref=bias_ref, s_scr=s_scr, p_scr=p_scr, mixt_scr=mixt_scr)
    for sub in range(STEP_TILES):
        r0 = sub * TILE
        xf = x_ref[r0:r0 + TILE, :]
        xb = xf.astype(bf16)
        qt = _dot_nt(w_qvt_ref[0:E_MIX, :], xb) * (LOG2E / math.sqrt(HEAD_DIM))
        qt_scr[...] = qt.astype(bf16).reshape(N_PAIRS, LANES, TILE)
        new = slice(HIST + r0, WIN + r0)
        vt = _dot_nt(w_qvt_ref[E_MIX:2 * E_MIX, :], xb)
        vt_scr[:, ONES_ROWS:ONES_ROWS + LANES, new] = vt.astype(bf16).reshape(N_PAIRS, LANES, TILE)
        ones = jnp.ones((N_PAIRS, ONES_ROWS, TILE), bf16)
        vt_scr[:, 0:ONES_ROWS, new] = ones
        vt_scr[:, ONES_ROWS + LANES:VT_ROWS, new] = ones
        k = _dot(xb, w_tok_ref[:, _AK0:_AK0 + E_MIX])
        for hp in range(N_PAIRS):
            k_scr[hp, new, 0:LANES] = k[:, hp * LANES:(hp + 1) * LANES].astype(bf16)

        core(r0)
        mix = jnp.concatenate([mixt_scr[hp].T for hp in range(N_PAIRS)], axis=-1)
        _layer_tail(xf, xb, mix, w_tok_ref[:, _AQM0:_AQM0 + E_MEM],
                    w_tok_ref[:, _AZ0:_AZ0 + E_BRANCH], w_out_ref, kvm_ref, u_scr, r0)


_CARRY = 8


def _conv_layer_kernel(x_ref, w_in_ref, w_out_ref, kvm_ref, cw_ref, g_ref, b_ref, o_ref,
                       u_scr, cu_scr, *, n_steps):
    si = lax.rem(pl.program_id(0), n_steps)
    _init_residual(u_scr)

    @pl.when(si == 0)
    def _():
        cu_scr[0:_CARRY, :] = jnp.zeros((_CARRY, E_MIX), f32)

    @pl.when(si > 0)
    def _():
        cu_scr[0:_CARRY, :] = cu_scr[STEP:STEP + _CARRY, :]

    _norm_previous_step(u_scr, g_ref, b_ref, o_ref)
    cw = cw_ref[...]
    for sub in range(STEP_TILES):
        r0 = sub * TILE
        xf = x_ref[r0:r0 + TILE, :]
        xb = xf.astype(bf16)
        p0 = _dot(xb, w_in_ref[:, _Q0:_Q0 + E_MIX])
        p1 = _dot(xb, w_in_ref[:, _K0:_K0 + E_MIX])
        p2 = _dot(xb, w_in_ref[:, _V0:_V0 + E_MIX])
        c0 = _CARRY + r0
        cu_scr[c0:c0 + TILE, :] = p1 * p2
        conv = cu_scr[c0:c0 + TILE, :] * cw[CONV_W - 1:CONV_W, :]
        for t in range(1, CONV_W):
            conv = conv + cu_scr[c0 - t:c0 - t + TILE, :] * cw[CONV_W - 1 - t:CONV_W - t, :]
        mix = p0 * conv
        _layer_tail(xf, xb, mix, w_in_ref[:, _QM0:_QM0 + E_MEM], w_in_ref[:, _Z0:_Z0 + E_BRANCH],
                    w_out_ref, kvm_ref, u_scr, r0)


def _const_spec(shape):
    nd = len(shape)
    return pl.BlockSpec(shape, lambda g: (0,) * nd, pipeline_mode=pl.Buffered(1))


def _layer_call(kind, x, weights, w_out_l, kvm_l, extra, g, b):
    batch, seq, _ = x.shape
    n_steps = seq // STEP
    last = batch * n_steps - 1

    def in_block(g):
        t = jnp.minimum(g, last)
        return t // n_steps, t % n_steps

    def out_block(g):
        t = jnp.maximum(g - 1, 0)
        return t // n_steps, t % n_steps

    x_spec = pl.BlockSpec((None, STEP, D_MODEL), lambda g: (*in_block(g), 0))
    kvm_spec = pl.BlockSpec((None, N_MEM, 2 * E_MEM), lambda g: (in_block(g)[0], 0, 0))
    w_specs = [_const_spec(w.shape) for w in weights]
    vec_spec = _const_spec((1, D_MODEL))
    tail_specs = [_const_spec((E_BRANCH, D_MODEL)), kvm_spec, _const_spec(extra.shape),
                  vec_spec, vec_spec]
    if kind == "attn":
        body = _attn_layer_kernel
        scratch = [
            pltpu.VMEM((STEP, D_MODEL), f32),
            pltpu.VMEM((N_PAIRS, LANES, TILE), bf16),
            pltpu.VMEM((N_PAIRS, KV_ROWS, K_LANES), bf16),
            pltpu.VMEM((N_PAIRS, VT_ROWS, KV_ROWS), bf16),
            pltpu.VMEM((HEADS_PER_PAIR, WIN, TILE), f32),
            pltpu.VMEM((HEADS_PER_PAIR, WIN, TILE), bf16),
            pltpu.VMEM((N_PAIRS, LANES, TILE), f32),
        ]
    else:
        body = _conv_layer_kernel
        scratch = [pltpu.VMEM((STEP, D_MODEL), f32),
                   pltpu.VMEM((STEP + _CARRY, E_MIX), f32)]
    return pl.pallas_call(
        functools.partial(body, n_steps=n_steps),
        grid=(batch * n_steps + 1,),
        in_specs=[x_spec] + w_specs + tail_specs,
        out_specs=pl.BlockSpec((None, STEP, D_MODEL), lambda g: (*out_block(g), 0)),
        out_shape=jax.ShapeDtypeStruct(x.shape, f32),
        scratch_shapes=scratch,
        compiler_params=pltpu.CompilerParams(
            dimension_semantics=("arbitrary",), vmem_limit_bytes=VMEM_LIMIT),
        name=f"{kind}_layer",
    )(x, *weights, w_out_l, kvm_l, extra, g, b)


def kernel(x, mem, w_in, w_mem_kv, w_out, rel_bias, conv_w, ln_g, ln_b):
    batch, seq, d = x.shape
    assert d == D_MODEL and seq % STEP == 0 and STEP >= HIST
    assert mem.shape == (batch, N_MEM, D_MODEL)
    w_out_bf = w_out.astype(bf16)
    kvm = _kv_mem_all(mem.reshape(batch * N_MEM, D_MODEL), w_mem_kv.astype(bf16))
    kvm = kvm.reshape(DEPTH, batch, N_MEM, 2 * E_MEM)
    tshift = jnp.pad(rel_bias[:, :, 1:], ((0, 0), (0, 0), (0, BASE_W - (N_REL - 1))))
    for layer in range(DEPTH):
        g = ln_g[layer].reshape(1, D_MODEL)
        b = ln_b[layer].reshape(1, D_MODEL)
        w_l = w_in[layer]
        if layer % 2 == 0:
            w_tok = jnp.concatenate([w_l[:, _K0:_K0 + E_MIX], w_l[:, _QM0:]], axis=1).astype(bf16)
            w_qvt = jnp.concatenate([w_l[:, _Q0:_Q0 + E_MIX], w_l[:, _V0:_V0 + E_MIX]],
                                    axis=1).T.astype(bf16)
            x = _layer_call("attn", x, [w_tok, w_qvt], w_out_bf[layer], kvm[layer],
                            _tile_bias(tshift[layer // 2]), g, b)
        else:
            x = _layer_call("conv", x, [w_l.astype(bf16)], w_out_bf[layer], kvm[layer],
                            conv_w[layer // 2], g, b)
    return x
```

```python
import functools
import math

import jax
import jax.numpy as jnp
from jax import lax
from jax.experimental import pallas as pl
from jax.experimental.pallas import tpu as pltpu

D_MODEL = 1024
DEPTH = 4
CHUNK = 64
N_PREV = 8
N_HEADS = 16
HEAD_DIM = 64
E_MIX = N_HEADS * HEAD_DIM
REL_CLIP = 128
N_REL = 2 * REL_CLIP + 1
CONV_W = 3
N_MEM = 256
MEM_HEADS = 4
MEM_HEAD_DIM = 128
E_MEM = MEM_HEADS * MEM_HEAD_DIM
E_BRANCH = E_MIX + E_MEM
DN_ALPHA = (2.0 * DEPTH) ** 0.25
LN_EPS = 1e-5

LANES = 128
TILE = 256
STEP_TILES = 2
STEP = STEP_TILES * TILE
HIST = N_PREV * CHUNK
WIN = HIST + TILE
KV_ROWS = HIST + STEP
N_PAIRS = E_MIX // LANES
HEADS_PER_PAIR = LANES // HEAD_DIM
ONES_ROWS = 16
VT_ROWS = LANES + 2 * ONES_ROWS
QK_ROWS = 128
K_LANES = 2 * LANES
HALF = TILE // 2
HALF_WIN = HIST + HALF
BASE_W = 384
NEG = -1e30
LOG2E = math.log2(math.e)
VMEM_BYTES = 64 * 1024 * 1024
VMEM_TEMP_BYTES = 8 * 1024 * 1024

_Q0, _K0, _V0 = 0, E_MIX, 2 * E_MIX
_QM0 = 3 * E_MIX
_Z0 = 3 * E_MIX + E_MEM
_AK0, _AQM0, _AZ0 = 0, E_MIX, E_MIX + E_MEM

bf16 = jnp.bfloat16
f32 = jnp.float32


def _nbytes(shape, dtype):
    return math.prod(shape) * jnp.dtype(dtype).itemsize


def _vmem_limit(windows, scratch=()):
    need = (sum(n * _nbytes(s, d) for s, d, n in windows)
            + sum(_nbytes(s, d) for s, d in scratch) + VMEM_TEMP_BYTES)
    assert need <= VMEM_BYTES, need
    return need


def _dot(a, b):
    return jnp.dot(a, b, preferred_element_type=f32)


def _dot_nt(a, b):
    return lax.dot_general(a, b, (((1,), (1,)), ((), ())), preferred_element_type=f32)


def _softmax2_pv(s, v):
    m = jnp.max(s, axis=-1, keepdims=True)
    p = jnp.exp2(s - m)
    l = jnp.sum(p, axis=-1, keepdims=True)
    return _dot(p.astype(bf16), v) / l


def _kvmem_kernel(mem_ref, w_ref, o_ref):
    o_ref[...] = _dot(mem_ref[...].astype(bf16), w_ref[...]).astype(bf16)


def _kv_mem_all(mem2d, w_mem_kv_bf):
    rows = mem2d.shape[0]
    return pl.pallas_call(
        _kvmem_kernel,
        grid=(DEPTH,),
        in_specs=[
            pl.BlockSpec((rows, D_MODEL), lambda l: (0, 0)),
            pl.BlockSpec((None, D_MODEL, 2 * E_MEM), lambda l: (l, 0, 0)),
        ],
        out_specs=pl.BlockSpec((None, rows, 2 * E_MEM), lambda l: (l, 0, 0)),
        out_shape=jax.ShapeDtypeStruct((DEPTH, rows, 2 * E_MEM), bf16),
        compiler_params=pltpu.CompilerParams(
            dimension_semantics=("arbitrary",),
            vmem_limit_bytes=_vmem_limit([((rows, D_MODEL), f32, 2),
                                          ((D_MODEL, 2 * E_MEM), bf16, 2),
                                          ((rows, 2 * E_MEM), bf16, 2)])),
        name="kv_mem_proj",
    )(mem2d, w_mem_kv_bf)


_FAR_ROWS = HIST - REL_CLIP


def _bias_kernel(tshift_ref, o_ref):
    c = lax.broadcasted_iota(jnp.int32, (N_HEADS, BASE_W), 1)
    ts = tshift_ref[...]
    last = ts[:, 2 * REL_CLIP - 1:2 * REL_CLIP]
    base = jnp.where(c >= 2 * REL_CLIP, last, ts) * LOG2E
    far = base[:, BASE_W - HALF:]

    qchunk = lax.broadcasted_iota(jnp.int32, (N_HEADS, HALF), 1) // CHUNK
    d_far = (lax.broadcasted_iota(jnp.int32, (N_HEADS, _FAR_ROWS, HALF), 1) // CHUNK
             - lax.broadcasted_iota(jnp.int32, (N_HEADS, _FAR_ROWS, HALF), 2) // CHUNK)
    o_ref[:, 0:_FAR_ROWS, :] = jnp.where(d_far >= 0, far[:, None, :], NEG)

    def row(p, carry):
        rolled = pltpu.roll(base, (p + (BASE_W - HALF_WIN + 1)) % BASE_W, 1)[:, 0:HALF]
        val = jnp.where(p // CHUNK - qchunk <= N_PREV, rolled, NEG)
        o_ref[:, pl.ds(p, 1), :] = val[:, None, :]
        return carry

    lax.fori_loop(_FAR_ROWS, HALF_WIN, row, 0, unroll=8)


def _tile_bias(tshift):
    return pl.pallas_call(
        _bias_kernel,
        out_shape=jax.ShapeDtypeStruct((N_HEADS, HALF_WIN, HALF), f32),
        compiler_params=pltpu.CompilerParams(
            vmem_limit_bytes=_vmem_limit([((N_HEADS, BASE_W), f32, 1),
                                          ((N_HEADS, HALF_WIN, HALF), f32, 1)])),
        name="rel_bias_tile",
    )(tshift)


def _init_residual(u_scr):
    @pl.when(pl.program_id(0) == 0)
    def _():
        u_scr[...] = jnp.zeros(u_scr.shape, f32)


def _norm_previous_step(u_scr, g_ref, b_ref, o_ref):
    u = u_scr[...]
    mu = jnp.mean(u, axis=-1, keepdims=True)
    uc = u - mu
    var = jnp.mean(uc * uc, axis=-1, keepdims=True)
    o_ref[...] = uc * lax.rsqrt(var + LN_EPS) * g_ref[...] + b_ref[...]


def _layer_tail(xf, xb, mix, w_qm, w_z, w_out_ref, kvm_ref, u_scr, r0):
    qm = _dot(xb, w_qm) * (LOG2E / math.sqrt(MEM_HEAD_DIM))
    mem_outs = []
    for h in range(MEM_HEADS):
        lo = h * MEM_HEAD_DIM
        qh = qm[:, lo:lo + MEM_HEAD_DIM].astype(bf16)
        kh = kvm_ref[:, lo:lo + MEM_HEAD_DIM]
        vh = kvm_ref[:, E_MEM + lo:E_MEM + lo + MEM_HEAD_DIM]
        mem_outs.append(_softmax2_pv(_dot_nt(qh, kh), vh))
    z = _dot(xb, w_z)
    gate = z * jax.nn.sigmoid(z)
    y = jnp.concatenate([mix] + mem_outs, axis=-1) * gate
    out = _dot(y.astype(bf16), w_out_ref[...])
    u_scr[r0:r0 + TILE, :] = DN_ALPHA * xf + out


def _attn_core(base, qt_scr, k_scr, vt_scr, bias_ref, s_scr, p_scr, mixt_scr):
    zeros = jnp.zeros((HEAD_DIM, TILE), bf16)
    flag_rows = jnp.where(lax.broadcasted_iota(jnp.int32, (LANES, TILE), 0) == 0, NEG, 0.0).astype(bf16)

    def scores(hp, e):
        lo = e * HEAD_DIM
        qth = qt_scr[hp, lo:lo + HEAD_DIM, :]
        qtm = jnp.concatenate([qth, zeros, flag_rows] if e == 0 else [zeros, qth, flag_rows], axis=0)
        h = HEADS_PER_PAIR * hp + e
        ms = [None, None]
        for a, b in [(r, r + QK_ROWS) for r in range(0, WIN, QK_ROWS)]:
            halves = [half for half in range(2)
                      if max(a, half * HALF) < min(b, half * HALF + HALF_WIN)]
            c0 = halves[0] * HALF
            st = _dot(k_scr[hp, base + a:base + b, :], qtm[:, c0:(halves[-1] + 1) * HALF])
            for half in halves:
                ra, rb = max(a, half * HALF), min(b, half * HALF + HALF_WIN)
                cols = slice(half * HALF, (half + 1) * HALF)
                sh = (st[ra - a:rb - a, half * HALF - c0:(half + 1) * HALF - c0]
                      + bias_ref[h, ra - half * HALF:rb - half * HALF, :])
                s_scr[e, ra:rb, cols] = sh
                mh = jnp.max(sh, axis=0, keepdims=True)
                ms[half] = mh if ms[half] is None else jnp.maximum(ms[half], mh)
        return ms

    def exponentials(e, ms):
        for half in range(2):
            ra, rb = half * HALF, half * HALF + HALF_WIN
            cols = slice(half * HALF, (half + 1) * HALF)
            p_scr[e, ra:rb, cols] = jnp.exp2(s_scr[e, ra:rb, cols] - ms[half]).astype(bf16)

    def weighted_values(hp, e):
        lo = e * HEAD_DIM
        r0 = e * (HEAD_DIM + ONES_ROWS)
        ot = _dot(vt_scr[hp, r0:r0 + HEAD_DIM + ONES_ROWS, base:base + WIN], p_scr[e])
        if e == 0:
            l, num = ot[0:1, :], ot[ONES_ROWS:, :]
        else:
            l, num = ot[HEAD_DIM:HEAD_DIM + 1, :], ot[0:HEAD_DIM, :]
        mixt_scr[hp, lo:lo + HEAD_DIM, :] = num / l

    def pair_body(hp, m1):
        weighted_values(hp, 0)
        m0_next = scores(hp + 1, 0)
        exponentials(1, m1)
        weighted_values(hp, 1)
        m1_next = scores(hp + 1, 1)
        exponentials(0, m0_next)
        return m1_next

    m0 = scores(0, 0)
    m1 = scores(0, 1)
    exponentials(0, m0)
    for hp in range(N_PAIRS - 1):
        m1 = pair_body(hp, m1)
    weighted_values(N_PAIRS - 1, 0)
    exponentials(1, m1)
    weighted_values(N_PAIRS - 1, 1)


def _attn_layer_kernel(x_ref, w_tok_ref, w_qvt_ref, w_out_ref, kvm_ref, bias_ref, g_ref, b_ref,
                       o_ref, u_scr, qt_scr, k_scr, vt_scr, s_scr, p_scr, mixt_scr, *, n_steps):
    si = lax.rem(pl.program_id(0), n_steps)
    _init_residual(u_scr)

    @pl.when(pl.program_id(0) == 0)
    def _():
        k_scr[:, :, LANES:K_LANES] = jnp.zeros((N_PAIRS, KV_ROWS, LANES), bf16)
        p_scr[:, HALF_WIN:WIN, 0:HALF] = jnp.zeros((HEADS_PER_PAIR, WIN - HALF_WIN, HALF), bf16)
        p_scr[:, 0:HALF, HALF:TILE] = jnp.zeros((HEADS_PER_PAIR, HALF, HALF), bf16)

    @pl.when(si == 0)
    def _():
        lane = lax.broadcasted_iota(jnp.int32, (N_PAIRS, HIST, K_LANES), 2)
        k_scr[:, 0:HIST, :] = jnp.where(lane == LANES, 1.0, 0.0).astype(bf16)
        vt_scr[:, :, 0:HIST] = jnp.zeros((N_PAIRS, VT_ROWS, HIST), bf16)

    @pl.when(si > 0)
    def _():
        k_scr[:, 0:HIST, :] = k_scr[:, STEP:KV_ROWS, :]
        vt_scr[:, :, 0:HIST] = vt_scr[:, :, STEP:KV_ROWS]

    _norm_previous_step(u_scr, g_ref, b_ref, o_ref)
    core = functools.partial(_attn_core, qt_scr=qt_scr, k_scr=k_scr, vt_scr=vt_scr,
                             bias_ref=bias_ref, s_scr=s_scr, p_scr=p_scr, mixt_scr=mixt_scr)
    for sub in range(STEP_TILES):
        r0 = sub * TILE
        xf = x_ref[r0:r0 + TILE, :]
        xb = xf.astype(bf16)
        qt = _dot_nt(w_qvt_ref[0:E_MIX, :], xb) * (LOG2E / math.sqrt(HEAD_DIM))
        qt_scr[...] = qt.astype(bf16).reshape(N_PAIRS, LANES, TILE)
        new = slice(HIST + r0, WIN + r0)
        vt = _dot_nt(w_qvt_ref[E_MIX:2 * E_MIX, :], xb)
        vt_scr[:, ONES_ROWS:ONES_ROWS + LANES, new] = vt.astype(bf16).reshape(N_PAIRS, LANES, TILE)
        ones = jnp.ones((N_PAIRS, ONES_ROWS, TILE), bf16)
        vt_scr[:, 0:ONES_ROWS, new] = ones
        vt_scr[:, ONES_ROWS + LANES:VT_ROWS, new] = ones
        k = _dot(xb, w_tok_ref[:, _AK0:_AK0 + E_MIX])
        for hp in range(N_PAIRS):
            k_scr[hp, new, 0:LANES] = k[:, hp * LANES:(hp + 1) * LANES].astype(bf16)

        core(r0)
        mix = jnp.concatenate([mixt_scr[hp].T for hp in range(N_PAIRS)], axis=-1)
        _layer_tail(xf, xb, mix, w_tok_ref[:, _AQM0:_AQM0 + E_MEM],
                    w_tok_ref[:, _AZ0:_AZ0 + E_BRANCH], w_out_ref, kvm_ref, u_scr, r0)


_CARRY = 8


def _conv_layer_kernel(x_ref, w_in_ref, w_out_ref, kvm_ref, cw_ref, g_ref, b_ref, o_ref,
                       u_scr, cu_scr, *, n_steps):
    si = lax.rem(pl.program_id(0), n_steps)
    _init_residual(u_scr)

    @pl.when(si == 0)
    def _():
        cu_scr[0:_CARRY, :] = jnp.zeros((_CARRY, E_MIX), f32)

    @pl.when(si > 0)
    def _():
        cu_scr[0:_CARRY, :] = cu_scr[STEP:STEP + _CARRY, :]

    _norm_previous_step(u_scr, g_ref, b_ref, o_ref)
    cw = cw_ref[...]
    for sub in range(STEP_TILES):
        r0 = sub * TILE
        xf = x_ref[r0:r0 + TILE, :]
        xb = xf.astype(bf16)
        p0 = _dot(xb, w_in_ref[:, _Q0:_Q0 + E_MIX])
        p1 = _dot(xb, w_in_ref[:, _K0:_K0 + E_MIX])
        p2 = _dot(xb, w_in_ref[:, _V0:_V0 + E_MIX])
        c0 = _CARRY + r0
        cu_scr[c0:c0 + TILE, :] = p1 * p2
        conv = cu_scr[c0:c0 + TILE, :] * cw[CONV_W - 1:CONV_W, :]
        for t in range(1, CONV_W):
            conv = conv + cu_scr[c0 - t:c0 - t + TILE, :] * cw[CONV_W - 1 - t:CONV_W - t, :]
        mix = p0 * conv
        _layer_tail(xf, xb, mix, w_in_ref[:, _QM0:_QM0 + E_MEM], w_in_ref[:, _Z0:_Z0 + E_BRANCH],
                    w_out_ref, kvm_ref, u_scr, r0)


def _layer_call(kind, x, weights, w_out_l, kvm_l, extra, g, b):
    batch, seq, _ = x.shape
    n_steps = seq // STEP
    last = batch * n_steps - 1

    def in_block(g):
        t = jnp.minimum(g, last)
        return t // n_steps, t % n_steps

    def out_block(g):
        t = jnp.maximum(g - 1, 0)
        return t // n_steps, t % n_steps

    def const(a):
        return a.shape, 1, lambda g: (0,) * a.ndim

    args = (x, *weights, w_out_l, kvm_l, extra, g, b)
    operands = ([((None, STEP, D_MODEL), 2, lambda g: (*in_block(g), 0))]
                + [const(w) for w in weights]
                + [const(w_out_l), ((None, N_MEM, 2 * E_MEM), 2, lambda g: (in_block(g)[0], 0, 0)),
                   const(extra), const(g), const(b)])
    if kind == "attn":
        body = _attn_layer_kernel
        scratch = [
            ((STEP, D_MODEL), f32),
            ((N_PAIRS, LANES, TILE), bf16),
            ((N_PAIRS, KV_ROWS, K_LANES), bf16),
            ((N_PAIRS, VT_ROWS, KV_ROWS), bf16),
            ((HEADS_PER_PAIR, WIN, TILE), f32),
            ((HEADS_PER_PAIR, WIN, TILE), bf16),
            ((N_PAIRS, LANES, TILE), f32),
        ]
    else:
        body = _conv_layer_kernel
        scratch = [((STEP, D_MODEL), f32), ((STEP + _CARRY, E_MIX), f32)]
    windows = [(tuple(d for d in shape if d is not None), a.dtype, n)
               for (shape, n, _), a in zip(operands, args)]
    windows.append(((STEP, D_MODEL), f32, 2))
    return pl.pallas_call(
        functools.partial(body, n_steps=n_steps),
        grid=(batch * n_steps + 1,),
        in_specs=[pl.BlockSpec(shape, imap, pipeline_mode=pl.Buffered(1) if n == 1 else None)
                  for shape, n, imap in operands],
        out_specs=pl.BlockSpec((None, STEP, D_MODEL), lambda g: (*out_block(g), 0)),
        out_shape=jax.ShapeDtypeStruct(x.shape, f32),
        scratch_shapes=[pltpu.VMEM(s, d) for s, d in scratch],
        compiler_params=pltpu.CompilerParams(
            dimension_semantics=("arbitrary",), vmem_limit_bytes=_vmem_limit(windows, scratch)),
        name=f"{kind}_layer",
    )(*args)


def kernel(x, mem, w_in, w_mem_kv, w_out, rel_bias, conv_w, ln_g, ln_b):
    batch, seq, d = x.shape
    assert d == D_MODEL and seq % STEP == 0 and STEP >= HIST
    assert mem.shape == (batch, N_MEM, D_MODEL)
    w_in_bf = w_in.astype(bf16)
    w_out_bf = w_out.astype(bf16)
    kvm = _kv_mem_all(mem.reshape(batch * N_MEM, D_MODEL), w_mem_kv.astype(bf16))
    kvm = kvm.reshape(DEPTH, batch, N_MEM, 2 * E_MEM)
    tshift = jnp.pad(rel_bias[:, :, 1:], ((0, 0), (0, 0), (0, BASE_W - (N_REL - 1))))
    for layer in range(DEPTH):
        g = ln_g[layer].reshape(1, D_MODEL)
        b = ln_b[layer].reshape(1, D_MODEL)
        w_l = w_in_bf[layer]
        if layer % 2 == 0:
            w_tok = jnp.concatenate([w_l[:, _K0:_K0 + E_MIX], w_l[:, _QM0:]], axis=1)
            w_qvt = jnp.concatenate([w_l[:, _Q0:_Q0 + E_MIX], w_l[:, _V0:_V0 + E_MIX]], axis=1).T
            x = _layer_call("attn", x, [w_tok, w_qvt], w_out_bf[layer], kvm[layer],
                            _tile_bias(tshift[layer // 2]), g, b)
        else:
            x = _layer_call("conv", x, [w_l], w_out_bf[layer], kvm[layer],
                            conv_w[layer // 2], g, b)
    return x
```

```python
import functools
import math

import jax
import jax.numpy as jnp
from jax import lax
from jax.experimental import pallas as pl
from jax.experimental.pallas import tpu as pltpu

D_MODEL = 1024
DEPTH = 4
CHUNK = 64
N_PREV = 8
N_HEADS = 16
HEAD_DIM = 64
E_MIX = N_HEADS * HEAD_DIM
REL_CLIP = 128
N_REL = 2 * REL_CLIP + 1
CONV_W = 3
N_MEM = 256
MEM_HEADS = 4
MEM_HEAD_DIM = 128
E_MEM = MEM_HEADS * MEM_HEAD_DIM
E_BRANCH = E_MIX + E_MEM
DN_ALPHA = (2.0 * DEPTH) ** 0.25
LN_EPS = 1e-5

LANES = 128
TILE = 256
STEP_TILES = 2
STEP = STEP_TILES * TILE
HIST = N_PREV * CHUNK
WIN = HIST + TILE
KV_ROWS = HIST + STEP
N_PAIRS = E_MIX // LANES
HEADS_PER_PAIR = LANES // HEAD_DIM
ONES_ROWS = 16
VT_ROWS = LANES + 2 * ONES_ROWS
QK_ROWS = 128
K_LANES = 2 * LANES
HALF = TILE // 2
HALF_WIN = HIST + HALF
BASE_W = 384
NEG = -1e30
LOG2E = math.log2(math.e)
VMEM_BYTES = 64 * 1024 * 1024
VMEM_TEMP_BYTES = 8 * 1024 * 1024
VMEM_RESERVE_BYTES = 4 * 1024 * 1024

_Q0, _K0, _V0 = 0, E_MIX, 2 * E_MIX
_QM0 = 3 * E_MIX
_Z0 = 3 * E_MIX + E_MEM
_AK0, _AQM0, _AZ0 = 0, E_MIX, E_MIX + E_MEM

bf16 = jnp.bfloat16
f32 = jnp.float32


def _nbytes(shape, dtype):
    return math.prod(shape) * jnp.dtype(dtype).itemsize


def _vmem_limit(windows, scratch=()):
    need = (sum(n * _nbytes(s, d) for s, d, n in windows)
            + sum(_nbytes(s, d) for s, d in scratch) + VMEM_TEMP_BYTES)
    limit = VMEM_BYTES - VMEM_RESERVE_BYTES
    assert need <= limit, need
    return limit


def _dot(a, b):
    return jnp.dot(a, b, preferred_element_type=f32)


def _dot_nt(a, b):
    return lax.dot_general(a, b, (((1,), (1,)), ((), ())), preferred_element_type=f32)


def _softmax2_pv(s, v):
    m = jnp.max(s, axis=-1, keepdims=True)
    p = jnp.exp2(s - m)
    l = jnp.sum(p, axis=-1, keepdims=True)
    return _dot(p.astype(bf16), v) / l


def _kvmem_kernel(mem_ref, w_ref, o_ref):
    o_ref[...] = _dot(mem_ref[...].astype(bf16), w_ref[...]).astype(bf16)


def _kv_mem_all(mem2d, w_mem_kv_bf):
    rows = mem2d.shape[0]
    return pl.pallas_call(
        _kvmem_kernel,
        grid=(DEPTH,),
        in_specs=[
            pl.BlockSpec((rows, D_MODEL), lambda l: (0, 0)),
            pl.BlockSpec((None, D_MODEL, 2 * E_MEM), lambda l: (l, 0, 0)),
        ],
        out_specs=pl.BlockSpec((None, rows, 2 * E_MEM), lambda l: (l, 0, 0)),
        out_shape=jax.ShapeDtypeStruct((DEPTH, rows, 2 * E_MEM), bf16),
        compiler_params=pltpu.CompilerParams(
            dimension_semantics=("arbitrary",),
            vmem_limit_bytes=_vmem_limit([((rows, D_MODEL), f32, 2),
                                          ((D_MODEL, 2 * E_MEM), bf16, 2),
                                          ((rows, 2 * E_MEM), bf16, 2)])),
        name="kv_mem_proj",
    )(mem2d, w_mem_kv_bf)


_FAR_ROWS = HIST - REL_CLIP


def _bias_kernel(tshift_ref, o_ref):
    c = lax.broadcasted_iota(jnp.int32, (N_HEADS, BASE_W), 1)
    ts = tshift_ref[...]
    last = ts[:, 2 * REL_CLIP - 1:2 * REL_CLIP]
    base = jnp.where(c >= 2 * REL_CLIP, last, ts) * LOG2E
    far = base[:, BASE_W - HALF:]

    qchunk = lax.broadcasted_iota(jnp.int32, (N_HEADS, HALF), 1) // CHUNK
    d_far = (lax.broadcasted_iota(jnp.int32, (N_HEADS, _FAR_ROWS, HALF), 1) // CHUNK
             - lax.broadcasted_iota(jnp.int32, (N_HEADS, _FAR_ROWS, HALF), 2) // CHUNK)
    o_ref[:, 0:_FAR_ROWS, :] = jnp.where(d_far >= 0, far[:, None, :], NEG)

    def row(p, carry):
        rolled = pltpu.roll(base, (p + (BASE_W - HALF_WIN + 1)) % BASE_W, 1)[:, 0:HALF]
        val = jnp.where(p // CHUNK - qchunk <= N_PREV, rolled, NEG)
        o_ref[:, pl.ds(p, 1), :] = val[:, None, :]
        return carry

    lax.fori_loop(_FAR_ROWS, HALF_WIN, row, 0, unroll=8)


def _tile_bias(tshift):
    return pl.pallas_call(
        _bias_kernel,
        out_shape=jax.ShapeDtypeStruct((N_HEADS, HALF_WIN, HALF), f32),
        compiler_params=pltpu.CompilerParams(
            vmem_limit_bytes=_vmem_limit([((N_HEADS, BASE_W), f32, 1),
                                          ((N_HEADS, HALF_WIN, HALF), f32, 1)])),
        name="rel_bias_tile",
    )(tshift)


def _init_residual(u_scr):
    @pl.when(pl.program_id(0) == 0)
    def _():
        u_scr[...] = jnp.zeros(u_scr.shape, f32)


def _norm_previous_step(u_scr, g_ref, b_ref, o_ref):
    u = u_scr[...]
    mu = jnp.mean(u, axis=-1, keepdims=True)
    uc = u - mu
    var = jnp.mean(uc * uc, axis=-1, keepdims=True)
    o_ref[...] = uc * lax.rsqrt(var + LN_EPS) * g_ref[...] + b_ref[...]


def _layer_tail(xf, xb, mix, w_qm, w_z, w_out_ref, kvm_ref, u_scr, r0):
    qm = _dot(xb, w_qm) * (LOG2E / math.sqrt(MEM_HEAD_DIM))
    mem_outs = []
    for h in range(MEM_HEADS):
        lo = h * MEM_HEAD_DIM
        qh = qm[:, lo:lo + MEM_HEAD_DIM].astype(bf16)
        kh = kvm_ref[:, lo:lo + MEM_HEAD_DIM]
        vh = kvm_ref[:, E_MEM + lo:E_MEM + lo + MEM_HEAD_DIM]
        mem_outs.append(_softmax2_pv(_dot_nt(qh, kh), vh))
    z = _dot(xb, w_z)
    gate = z * jax.nn.sigmoid(z)
    y = jnp.concatenate([mix] + mem_outs, axis=-1) * gate
    out = _dot(y.astype(bf16), w_out_ref[...])
    u_scr[r0:r0 + TILE, :] = DN_ALPHA * xf + out


def _attn_core(base, qt_scr, k_scr, vt_scr, bias_ref, s_scr, p_scr, mixt_scr):
    zeros = jnp.zeros((HEAD_DIM, TILE), bf16)
    flag_rows = jnp.where(lax.broadcasted_iota(jnp.int32, (LANES, TILE), 0) == 0, NEG, 0.0).astype(bf16)

    def scores(hp, e):
        lo = e * HEAD_DIM
        qth = qt_scr[hp, lo:lo + HEAD_DIM, :]
        qtm = jnp.concatenate([qth, zeros, flag_rows] if e == 0 else [zeros, qth, flag_rows], axis=0)
        h = HEADS_PER_PAIR * hp + e
        ms = [None, None]
        for a, b in [(r, r + QK_ROWS) for r in range(0, WIN, QK_ROWS)]:
            halves = [half for half in range(2)
                      if max(a, half * HALF) < min(b, half * HALF + HALF_WIN)]
            c0 = halves[0] * HALF
            st = _dot(k_scr[hp, base + a:base + b, :], qtm[:, c0:(halves[-1] + 1) * HALF])
            for half in halves:
                ra, rb = max(a, half * HALF), min(b, half * HALF + HALF_WIN)
                cols = slice(half * HALF, (half + 1) * HALF)
                sh = (st[ra - a:rb - a, half * HALF - c0:(half + 1) * HALF - c0]
                      + bias_ref[h, ra - half * HALF:rb - half * HALF, :])
                s_scr[e, ra:rb, cols] = sh
                mh = jnp.max(sh, axis=0, keepdims=True)
                ms[half] = mh if ms[half] is None else jnp.maximum(ms[half], mh)
        return ms

    def exponentials(e, ms):
        for half in range(2):
            ra, rb = half * HALF, half * HALF + HALF_WIN
            cols = slice(half * HALF, (half + 1) * HALF)
            p_scr[e, ra:rb, cols] = jnp.exp2(s_scr[e, ra:rb, cols] - ms[half]).astype(bf16)

    def weighted_values(hp, e):
        lo = e * HEAD_DIM
        r0 = e * (HEAD_DIM + ONES_ROWS)
        ot = _dot(vt_scr[hp, r0:r0 + HEAD_DIM + ONES_ROWS, base:base + WIN], p_scr[e])
        if e == 0:
            l, num = ot[0:1, :], ot[ONES_ROWS:, :]
        else:
            l, num = ot[HEAD_DIM:HEAD_DIM + 1, :], ot[0:HEAD_DIM, :]
        mixt_scr[hp, lo:lo + HEAD_DIM, :] = num / l

    def pair_body(hp, m1):
        weighted_values(hp, 0)
        m0_next = scores(hp + 1, 0)
        exponentials(1, m1)
        weighted_values(hp, 1)
        m1_next = scores(hp + 1, 1)
        exponentials(0, m0_next)
        return m1_next

    m0 = scores(0, 0)
    m1 = scores(0, 1)
    exponentials(0, m0)
    for hp in range(N_PAIRS - 1):
        m1 = pair_body(hp, m1)
    weighted_values(N_PAIRS - 1, 0)
    exponentials(1, m1)
    weighted_values(N_PAIRS - 1, 1)


def _attn_layer_kernel(x_ref, w_tok_ref, w_qvt_ref, w_out_ref, kvm_ref, bias_ref, g_ref, b_ref,
                       o_ref, u_scr, qt_scr, k_scr, vt_scr, s_scr, p_scr, mixt_scr, *, n_steps):
    si = lax.rem(pl.program_id(0), n_steps)
    _init_residual(u_scr)

    @pl.when(pl.program_id(0) == 0)
    def _():
        k_scr[:, :, LANES:K_LANES] = jnp.zeros((N_PAIRS, KV_ROWS, LANES), bf16)
        p_scr[:, HALF_WIN:WIN, 0:HALF] = jnp.zeros((HEADS_PER_PAIR, WIN - HALF_WIN, HALF), bf16)
        p_scr[:, 0:HALF, HALF:TILE] = jnp.zeros((HEADS_PER_PAIR, HALF, HALF), bf16)

    @pl.when(si == 0)
    def _():
        lane = lax.broadcasted_iota(jnp.int32, (N_PAIRS, HIST, K_LANES), 2)
        k_scr[:, 0:HIST, :] = jnp.where(lane == LANES, 1.0, 0.0).astype(bf16)
        vt_scr[:, :, 0:HIST] = jnp.zeros((N_PAIRS, VT_ROWS, HIST), bf16)

    @pl.when(si > 0)
    def _():
        k_scr[:, 0:HIST, :] = k_scr[:, STEP:KV_ROWS, :]
        vt_scr[:, :, 0:HIST] = vt_scr[:, :, STEP:KV_ROWS]

    _norm_previous_step(u_scr, g_ref, b_ref, o_ref)
    core = functools.partial(_attn_core, qt_scr=qt_scr, k_scr=k_scr, vt_scr=vt_scr,
                             bias_ref=bias_ref, s_scr=s_scr, p_scr=p_scr, mixt_scr=mixt_scr)
    for sub in range(STEP_TILES):
        r0 = sub * TILE
        xf = x_ref[r0:r0 + TILE, :]
        xb = xf.astype(bf16)
        qt = _dot_nt(w_qvt_ref[0:E_MIX, :], xb) * (LOG2E / math.sqrt(HEAD_DIM))
        qt_scr[...] = qt.astype(bf16).reshape(N_PAIRS, LANES, TILE)
        new = slice(HIST + r0, WIN + r0)
        vt = _dot_nt(w_qvt_ref[E_MIX:2 * E_MIX, :], xb)
        vt_scr[:, ONES_ROWS:ONES_ROWS + LANES, new] = vt.astype(bf16).reshape(N_PAIRS, LANES, TILE)
        ones = jnp.ones((N_PAIRS, ONES_ROWS, TILE), bf16)
        vt_scr[:, 0:ONES_ROWS, new] = ones
        vt_scr[:, ONES_ROWS + LANES:VT_ROWS, new] = ones
        k = _dot(xb, w_tok_ref[:, _AK0:_AK0 + E_MIX])
        for hp in range(N_PAIRS):
            k_scr[hp, new, 0:LANES] = k[:, hp * LANES:(hp + 1) * LANES].astype(bf16)

        core(r0)
        mix = jnp.concatenate([mixt_scr[hp].T for hp in range(N_PAIRS)], axis=-1)
        _layer_tail(xf, xb, mix, w_tok_ref[:, _AQM0:_AQM0 + E_MEM],
                    w_tok_ref[:, _AZ0:_AZ0 + E_BRANCH], w_out_ref, kvm_ref, u_scr, r0)


_CARRY = 8


def _conv_layer_kernel(x_ref, w_in_ref, w_out_ref, kvm_ref, cw_ref, g_ref, b_ref, o_ref,
                       u_scr, cu_scr, *, n_steps):
    si = lax.rem(pl.program_id(0), n_steps)
    _init_residual(u_scr)

    @pl.when(si == 0)
    def _():
        cu_scr[0:_CARRY, :] = jnp.zeros((_CARRY, E_MIX), f32)

    @pl.when(si > 0)
    def _():
        cu_scr[0:_CARRY, :] = cu_scr[STEP:STEP + _CARRY, :]

    _norm_previous_step(u_scr, g_ref, b_ref, o_ref)
    cw = cw_ref[...]
    for sub in range(STEP_TILES):
        r0 = sub * TILE
        xf = x_ref[r0:r0 + TILE, :]
        xb = xf.astype(bf16)
        p0 = _dot(xb, w_in_ref[:, _Q0:_Q0 + E_MIX])
        p1 = _dot(xb, w_in_ref[:, _K0:_K0 + E_MIX])
        p2 = _dot(xb, w_in_ref[:, _V0:_V0 + E_MIX])
        c0 = _CARRY + r0
        cu_scr[c0:c0 + TILE, :] = p1 * p2
        conv = cu_scr[c0:c0 + TILE, :] * cw[CONV_W - 1:CONV_W, :]
        for t in range(1, CONV_W):
            conv = conv + cu_scr[c0 - t:c0 - t + TILE, :] * cw[CONV_W - 1 - t:CONV_W - t, :]
        mix = p0 * conv
        _layer_tail(xf, xb, mix, w_in_ref[:, _QM0:_QM0 + E_MEM], w_in_ref[:, _Z0:_Z0 + E_BRANCH],
                    w_out_ref, kvm_ref, u_scr, r0)


def _layer_call(kind, x, weights, w_out_l, kvm_l, extra, g, b):
    batch, seq, _ = x.shape
    n_steps = seq // STEP
    last = batch * n_steps - 1

    def in_block(g):
        t = jnp.minimum(g, last)
        return t // n_steps, t % n_steps

    def out_block(g):
        t = jnp.maximum(g - 1, 0)
        return t // n_steps, t % n_steps

    def const(a):
        return a.shape, 1, lambda g: (0,) * a.ndim

    args = (x, *weights, w_out_l, kvm_l, extra, g, b)
    operands = ([((None, STEP, D_MODEL), 2, lambda g: (*in_block(g), 0))]
                + [const(w) for w in weights]
                + [const(w_out_l), ((None, N_MEM, 2 * E_MEM), 2, lambda g: (in_block(g)[0], 0, 0)),
                   const(extra), const(g), const(b)])
    if kind == "attn":
        body = _attn_layer_kernel
        scratch = [
            ((STEP, D_MODEL), f32),
            ((N_PAIRS, LANES, TILE), bf16),
            ((N_PAIRS, KV_ROWS, K_LANES), bf16),
            ((N_PAIRS, VT_ROWS, KV_ROWS), bf16),
            ((HEADS_PER_PAIR, WIN, TILE), f32),
            ((HEADS_PER_PAIR, WIN, TILE), bf16),
            ((N_PAIRS, LANES, TILE), f32),
        ]
    else:
        body = _conv_layer_kernel
        scratch = [((STEP, D_MODEL), f32), ((STEP + _CARRY, E_MIX), f32)]
    windows = [(tuple(d for d in shape if d is not None), a.dtype, n)
               for (shape, n, _), a in zip(operands, args)]
    windows.append(((STEP, D_MODEL), f32, 2))
    return pl.pallas_call(
        functools.partial(body, n_steps=n_steps),
        grid=(batch * n_steps + 1,),
        in_specs=[pl.BlockSpec(shape, imap, pipeline_mode=pl.Buffered(1) if n == 1 else None)
                  for shape, n, imap in operands],
        out_specs=pl.BlockSpec((None, STEP, D_MODEL), lambda g: (*out_block(g), 0)),
        out_shape=jax.ShapeDtypeStruct(x.shape, f32),
        scratch_shapes=[pltpu.VMEM(s, d) for s, d in scratch],
        compiler_params=pltpu.CompilerParams(
            dimension_semantics=("arbitrary",), vmem_limit_bytes=_vmem_limit(windows, scratch)),
        name=f"{kind}_layer",
    )(*args)


def kernel(x, mem, w_in, w_mem_kv, w_out, rel_bias, conv_w, ln_g, ln_b):
    batch, seq, d = x.shape
    assert d == D_MODEL and seq % STEP == 0 and STEP >= HIST
    assert mem.shape == (batch, N_MEM, D_MODEL)
    w_in_bf = w_in.astype(bf16)
    w_out_bf = w_out.astype(bf16)
    kvm = _kv_mem_all(mem.reshape(batch * N_MEM, D_MODEL), w_mem_kv.astype(bf16))
    kvm = kvm.reshape(DEPTH, batch, N_MEM, 2 * E_MEM)
    tshift = jnp.pad(rel_bias[:, :, 1:], ((0, 0), (0, 0), (0, BASE_W - (N_REL - 1))))
    for layer in range(DEPTH):
        g = ln_g[layer].reshape(1, D_MODEL)
        b = ln_b[layer].reshape(1, D_MODEL)
        w_l = w_in_bf[layer]
        if layer % 2 == 0:
            w_tok = jnp.concatenate([w_l[:, _K0:_K0 + E_MIX], w_l[:, _QM0:]], axis=1)
            w_qvt = jnp.concatenate([w_l[:, _Q0:_Q0 + E_MIX], w_l[:, _V0:_V0 + E_MIX]], axis=1).T
            x = _layer_call("attn", x, [w_tok, w_qvt], w_out_bf[layer], kvm[layer],
                            _tile_bias(tshift[layer // 2]), g, b)
        else:
            x = _layer_call("conv", x, [w_l], w_out_bf[layer], kvm[layer],
                            conv_w[layer // 2], g, b)
    return x
```

```python
import functools
import math

import jax
import jax.numpy as jnp
from jax import lax
from jax.experimental import pallas as pl
from jax.experimental.pallas import tpu as pltpu

D_MODEL = 1024
DEPTH = 4
CHUNK = 64
N_PREV = 8
N_HEADS = 16
HEAD_DIM = 64
E_MIX = N_HEADS * HEAD_DIM
REL_CLIP = 128
N_REL = 2 * REL_CLIP + 1
CONV_W = 3
N_MEM = 256
MEM_HEADS = 4
MEM_HEAD_DIM = 128
E_MEM = MEM_HEADS * MEM_HEAD_DIM
E_BRANCH = E_MIX + E_MEM
DN_ALPHA = (2.0 * DEPTH) ** 0.25
LN_EPS = 1e-5

LANES = 128
TILE = 256
STEP_TILES = 4
STEP = STEP_TILES * TILE
HIST = N_PREV * CHUNK
WIN = HIST + TILE
KV_ROWS = HIST + STEP
N_PAIRS = E_MIX // LANES
HEADS_PER_PAIR = LANES // HEAD_DIM
ONES_ROWS = 16
VT_ROWS = LANES + 2 * ONES_ROWS
QK_ROWS = 128
K_LANES = 2 * LANES
HALF = TILE // 2
HALF_WIN = HIST + HALF
BASE_W = 384
NEG = -1e30
LOG2E = math.log2(math.e)
VMEM_BYTES = 64 * 1024 * 1024
VMEM_TEMP_BYTES = 8 * 1024 * 1024
VMEM_RESERVE_BYTES = 2 * 1024 * 1024

_Q0, _K0, _V0 = 0, E_MIX, 2 * E_MIX
_QM0 = 3 * E_MIX
_Z0 = 3 * E_MIX + E_MEM
_AK0, _AQM0, _AZ0 = 0, E_MIX, E_MIX + E_MEM

bf16 = jnp.bfloat16
f32 = jnp.float32


def _nbytes(shape, dtype):
    return math.prod(shape) * jnp.dtype(dtype).itemsize


def _vmem_limit(windows, scratch=()):
    need = (sum(n * _nbytes(s, d) for s, d, n in windows)
            + sum(_nbytes(s, d) for s, d in scratch) + VMEM_TEMP_BYTES)
    limit = VMEM_BYTES - VMEM_RESERVE_BYTES
    assert need <= limit, need
    return limit


def _dot(a, b):
    return jnp.dot(a, b, preferred_element_type=f32)


def _dot_nt(a, b):
    return lax.dot_general(a, b, (((1,), (1,)), ((), ())), preferred_element_type=f32)


def _softmax2_pv(s, v):
    m = jnp.max(s, axis=-1, keepdims=True)
    p = jnp.exp2(s - m)
    l = jnp.sum(p, axis=-1, keepdims=True)
    return _dot(p.astype(bf16), v) / l


def _kvmem_kernel(mem_ref, w_ref, o_ref):
    o_ref[...] = _dot(mem_ref[...].astype(bf16), w_ref[...]).astype(bf16)


def _kv_mem_all(mem2d, w_mem_kv_bf):
    rows = mem2d.shape[0]
    return pl.pallas_call(
        _kvmem_kernel,
        grid=(DEPTH,),
        in_specs=[
            pl.BlockSpec((rows, D_MODEL), lambda l: (0, 0)),
            pl.BlockSpec((None, D_MODEL, 2 * E_MEM), lambda l: (l, 0, 0)),
        ],
        out_specs=pl.BlockSpec((None, rows, 2 * E_MEM), lambda l: (l, 0, 0)),
        out_shape=jax.ShapeDtypeStruct((DEPTH, rows, 2 * E_MEM), bf16),
        compiler_params=pltpu.CompilerParams(
            dimension_semantics=("arbitrary",),
            vmem_limit_bytes=_vmem_limit([((rows, D_MODEL), f32, 2),
                                          ((D_MODEL, 2 * E_MEM), bf16, 2),
                                          ((rows, 2 * E_MEM), bf16, 2)])),
        name="kv_mem_proj",
    )(mem2d, w_mem_kv_bf)


_FAR_ROWS = HIST - REL_CLIP


def _bias_kernel(tshift_ref, o_ref):
    c = lax.broadcasted_iota(jnp.int32, (N_HEADS, BASE_W), 1)
    ts = tshift_ref[...]
    last = ts[:, 2 * REL_CLIP - 1:2 * REL_CLIP]
    base = jnp.where(c >= 2 * REL_CLIP, last, ts) * LOG2E
    far = base[:, BASE_W - HALF:]

    qchunk = lax.broadcasted_iota(jnp.int32, (N_HEADS, HALF), 1) // CHUNK
    d_far = (lax.broadcasted_iota(jnp.int32, (N_HEADS, _FAR_ROWS, HALF), 1) // CHUNK
             - lax.broadcasted_iota(jnp.int32, (N_HEADS, _FAR_ROWS, HALF), 2) // CHUNK)
    o_ref[:, 0:_FAR_ROWS, :] = jnp.where(d_far >= 0, far[:, None, :], NEG)

    def row(p, carry):
        rolled = pltpu.roll(base, (p + (BASE_W - HALF_WIN + 1)) % BASE_W, 1)[:, 0:HALF]
        val = jnp.where(p // CHUNK - qchunk <= N_PREV, rolled, NEG)
        o_ref[:, pl.ds(p, 1), :] = val[:, None, :]
        return carry

    lax.fori_loop(_FAR_ROWS, HALF_WIN, row, 0, unroll=8)


def _tile_bias(tshift):
    return pl.pallas_call(
        _bias_kernel,
        out_shape=jax.ShapeDtypeStruct((N_HEADS, HALF_WIN, HALF), f32),
        compiler_params=pltpu.CompilerParams(
            vmem_limit_bytes=_vmem_limit([((N_HEADS, BASE_W), f32, 1),
                                          ((N_HEADS, HALF_WIN, HALF), f32, 1)])),
        name="rel_bias_tile",
    )(tshift)


def _init_residual(u_scr):
    @pl.when(pl.program_id(0) == 0)
    def _():
        u_scr[...] = jnp.zeros(u_scr.shape, f32)


def _norm_previous_step(u_scr, g_ref, b_ref, o_ref):
    u = u_scr[...]
    mu = jnp.mean(u, axis=-1, keepdims=True)
    uc = u - mu
    var = jnp.mean(uc * uc, axis=-1, keepdims=True)
    o_ref[...] = uc * lax.rsqrt(var + LN_EPS) * g_ref[...] + b_ref[...]


def _layer_tail(xf, xb, mix, w_qm, w_z, w_out_ref, kvm_ref, u_scr, r0):
    qm = _dot(xb, w_qm) * (LOG2E / math.sqrt(MEM_HEAD_DIM))
    mem_outs = []
    for h in range(MEM_HEADS):
        lo = h * MEM_HEAD_DIM
        qh = qm[:, lo:lo + MEM_HEAD_DIM].astype(bf16)
        kh = kvm_ref[:, lo:lo + MEM_HEAD_DIM]
        vh = kvm_ref[:, E_MEM + lo:E_MEM + lo + MEM_HEAD_DIM]
        mem_outs.append(_softmax2_pv(_dot_nt(qh, kh), vh))
    z = _dot(xb, w_z)
    gate = z * jax.nn.sigmoid(z)
    y = jnp.concatenate([mix] + mem_outs, axis=-1) * gate
    out = _dot(y.astype(bf16), w_out_ref[...])
    u_scr[r0:r0 + TILE, :] = DN_ALPHA * xf + out


def _attn_core(base, qt_scr, k_scr, vt_scr, bias_ref, s_scr, p_scr, mixt_scr):
    zeros = jnp.zeros((HEAD_DIM, TILE), bf16)
    flag_rows = jnp.where(lax.broadcasted_iota(jnp.int32, (LANES, TILE), 0) == 0, NEG, 0.0).astype(bf16)

    def scores(hp, e):
        lo = e * HEAD_DIM
        qth = qt_scr[hp, lo:lo + HEAD_DIM, :]
        qtm = jnp.concatenate([qth, zeros, flag_rows] if e == 0 else [zeros, qth, flag_rows], axis=0)
        h = HEADS_PER_PAIR * hp + e
        ms = [None, None]
        for a, b in [(r, r + QK_ROWS) for r in range(0, WIN, QK_ROWS)]:
            halves = [half for half in range(2)
                      if max(a, half * HALF) < min(b, half * HALF + HALF_WIN)]
            c0 = halves[0] * HALF
            st = _dot(k_scr[hp, base + a:base + b, :], qtm[:, c0:(halves[-1] + 1) * HALF])
            for half in halves:
                ra, rb = max(a, half * HALF), min(b, half * HALF + HALF_WIN)
                cols = slice(half * HALF, (half + 1) * HALF)
                sh = (st[ra - a:rb - a, half * HALF - c0:(half + 1) * HALF - c0]
                      + bias_ref[h, ra - half * HALF:rb - half * HALF, :])
                s_scr[e, ra:rb, cols] = sh
                mh = jnp.max(sh, axis=0, keepdims=True)
                ms[half] = mh if ms[half] is None else jnp.maximum(ms[half], mh)
        return ms

    def exponentials(e, ms):
        for half in range(2):
            ra, rb = half * HALF, half * HALF + HALF_WIN
            cols = slice(half * HALF, (half + 1) * HALF)
            p_scr[e, ra:rb, cols] = jnp.exp2(s_scr[e, ra:rb, cols] - ms[half]).astype(bf16)

    def weighted_values(hp, e):
        lo = e * HEAD_DIM
        r0 = e * (HEAD_DIM + ONES_ROWS)
        ot = _dot(vt_scr[hp, r0:r0 + HEAD_DIM + ONES_ROWS, base:base + WIN], p_scr[e])
        if e == 0:
            l, num = ot[0:1, :], ot[ONES_ROWS:, :]
        else:
            l, num = ot[HEAD_DIM:HEAD_DIM + 1, :], ot[0:HEAD_DIM, :]
        mixt_scr[hp, lo:lo + HEAD_DIM, :] = num / l

    def pair_body(hp, m1):
        weighted_values(hp, 0)
        m0_next = scores(hp + 1, 0)
        exponentials(1, m1)
        weighted_values(hp, 1)
        m1_next = scores(hp + 1, 1)
        exponentials(0, m0_next)
        return m1_next

    m0 = scores(0, 0)
    m1 = scores(0, 1)
    exponentials(0, m0)
    for hp in range(N_PAIRS - 1):
        m1 = pair_body(hp, m1)
    weighted_values(N_PAIRS - 1, 0)
    exponentials(1, m1)
    weighted_values(N_PAIRS - 1, 1)


def _attn_layer_kernel(x_ref, w_tok_ref, w_qvt_ref, w_out_ref, kvm_ref, bias_ref, g_ref, b_ref,
                       o_ref, u_scr, qt_scr, k_scr, vt_scr, s_scr, p_scr, mixt_scr, *, n_steps):
    si = lax.rem(pl.program_id(0), n_steps)
    _init_residual(u_scr)

    @pl.when(pl.program_id(0) == 0)
    def _():
        k_scr[:, :, LANES:K_LANES] = jnp.zeros((N_PAIRS, KV_ROWS, LANES), bf16)
        p_scr[:, HALF_WIN:WIN, 0:HALF] = jnp.zeros((HEADS_PER_PAIR, WIN - HALF_WIN, HALF), bf16)
        p_scr[:, 0:HALF, HALF:TILE] = jnp.zeros((HEADS_PER_PAIR, HALF, HALF), bf16)

    @pl.when(si == 0)
    def _():
        lane = lax.broadcasted_iota(jnp.int32, (N_PAIRS, HIST, K_LANES), 2)
        k_scr[:, 0:HIST, :] = jnp.where(lane == LANES, 1.0, 0.0).astype(bf16)
        vt_scr[:, :, 0:HIST] = jnp.zeros((N_PAIRS, VT_ROWS, HIST), bf16)

    @pl.when(si > 0)
    def _():
        k_scr[:, 0:HIST, :] = k_scr[:, STEP:KV_ROWS, :]
        vt_scr[:, :, 0:HIST] = vt_scr[:, :, STEP:KV_ROWS]

    _norm_previous_step(u_scr, g_ref, b_ref, o_ref)
    core = functools.partial(_attn_core, qt_scr=qt_scr, k_scr=k_scr, vt_scr=vt_scr,
                             bias_ref=bias_ref, s_scr=s_scr, p_scr=p_scr, mixt_scr=mixt_scr)
    for sub in range(STEP_TILES):
        r0 = sub * TILE
        xf = x_ref[r0:r0 + TILE, :]
        xb = xf.astype(bf16)
        qt = _dot_nt(w_qvt_ref[0:E_MIX, :], xb) * (LOG2E / math.sqrt(HEAD_DIM))
        qt_scr[...] = qt.astype(bf16).reshape(N_PAIRS, LANES, TILE)
        new = slice(HIST + r0, WIN + r0)
        vt = _dot_nt(w_qvt_ref[E_MIX:2 * E_MIX, :], xb)
        vt_scr[:, ONES_ROWS:ONES_ROWS + LANES, new] = vt.astype(bf16).reshape(N_PAIRS, LANES, TILE)
        ones = jnp.ones((N_PAIRS, ONES_ROWS, TILE), bf16)
        vt_scr[:, 0:ONES_ROWS, new] = ones
        vt_scr[:, ONES_ROWS + LANES:VT_ROWS, new] = ones
        k = _dot(xb, w_tok_ref[:, _AK0:_AK0 + E_MIX])
        for hp in range(N_PAIRS):
            k_scr[hp, new, 0:LANES] = k[:, hp * LANES:(hp + 1) * LANES].astype(bf16)

        core(r0)
        mix = jnp.concatenate([mixt_scr[hp].T for hp in range(N_PAIRS)], axis=-1)
        _layer_tail(xf, xb, mix, w_tok_ref[:, _AQM0:_AQM0 + E_MEM],
                    w_tok_ref[:, _AZ0:_AZ0 + E_BRANCH], w_out_ref, kvm_ref, u_scr, r0)


_CARRY = 8


def _conv_layer_kernel(x_ref, w_in_ref, w_out_ref, kvm_ref, cw_ref, g_ref, b_ref, o_ref,
                       u_scr, cu_scr, *, n_steps):
    si = lax.rem(pl.program_id(0), n_steps)
    _init_residual(u_scr)

    @pl.when(si == 0)
    def _():
        cu_scr[0:_CARRY, :] = jnp.zeros((_CARRY, E_MIX), f32)

    @pl.when(si > 0)
    def _():
        cu_scr[0:_CARRY, :] = cu_scr[STEP:STEP + _CARRY, :]

    _norm_previous_step(u_scr, g_ref, b_ref, o_ref)
    cw = cw_ref[...]
    for sub in range(STEP_TILES):
        r0 = sub * TILE
        xf = x_ref[r0:r0 + TILE, :]
        xb = xf.astype(bf16)
        p0 = _dot(xb, w_in_ref[:, _Q0:_Q0 + E_MIX])
        p1 = _dot(xb, w_in_ref[:, _K0:_K0 + E_MIX])
        p2 = _dot(xb, w_in_ref[:, _V0:_V0 + E_MIX])
        c0 = _CARRY + r0
        cu_scr[c0:c0 + TILE, :] = p1 * p2
        conv = cu_scr[c0:c0 + TILE, :] * cw[CONV_W - 1:CONV_W, :]
        for t in range(1, CONV_W):
            conv = conv + cu_scr[c0 - t:c0 - t + TILE, :] * cw[CONV_W - 1 - t:CONV_W - t, :]
        mix = p0 * conv
        _layer_tail(xf, xb, mix, w_in_ref[:, _QM0:_QM0 + E_MEM], w_in_ref[:, _Z0:_Z0 + E_BRANCH],
                    w_out_ref, kvm_ref, u_scr, r0)


def _layer_call(kind, x, weights, w_out_l, kvm_l, extra, g, b):
    batch, seq, _ = x.shape
    n_steps = seq // STEP
    last = batch * n_steps - 1

    def in_block(g):
        t = jnp.minimum(g, last)
        return t // n_steps, t % n_steps

    def out_block(g):
        t = jnp.maximum(g - 1, 0)
        return t // n_steps, t % n_steps

    def const(a):
        return a.shape, 1, lambda g: (0,) * a.ndim

    args = (x, *weights, w_out_l, kvm_l, extra, g, b)
    operands = ([((None, STEP, D_MODEL), 2, lambda g: (*in_block(g), 0))]
                + [const(w) for w in weights]
                + [const(w_out_l), ((None, N_MEM, 2 * E_MEM), 2, lambda g: (in_block(g)[0], 0, 0)),
                   const(extra), const(g), const(b)])
    if kind == "attn":
        body = _attn_layer_kernel
        scratch = [
            ((STEP, D_MODEL), f32),
            ((N_PAIRS, LANES, TILE), bf16),
            ((N_PAIRS, KV_ROWS, K_LANES), bf16),
            ((N_PAIRS, VT_ROWS, KV_ROWS), bf16),
            ((HEADS_PER_PAIR, WIN, TILE), f32),
            ((HEADS_PER_PAIR, WIN, TILE), bf16),
            ((N_PAIRS, LANES, TILE), f32),
        ]
    else:
        body = _conv_layer_kernel
        scratch = [((STEP, D_MODEL), f32), ((STEP + _CARRY, E_MIX), f32)]
    windows = [(tuple(d for d in shape if d is not None), a.dtype, n)
               for (shape, n, _), a in zip(operands, args)]
    windows.append(((STEP, D_MODEL), f32, 2))
    return pl.pallas_call(
        functools.partial(body, n_steps=n_steps),
        grid=(batch * n_steps + 1,),
        in_specs=[pl.BlockSpec(shape, imap, pipeline_mode=pl.Buffered(1) if n == 1 else None)
                  for shape, n, imap in operands],
        out_specs=pl.BlockSpec((None, STEP, D_MODEL), lambda g: (*out_block(g), 0)),
        out_shape=jax.ShapeDtypeStruct(x.shape, f32),
        scratch_shapes=[pltpu.VMEM(s, d) for s, d in scratch],
        compiler_params=pltpu.CompilerParams(
            dimension_semantics=("arbitrary",), vmem_limit_bytes=_vmem_limit(windows, scratch)),
        name=f"{kind}_layer",
    )(*args)


def kernel(x, mem, w_in, w_mem_kv, w_out, rel_bias, conv_w, ln_g, ln_b):
    batch, seq, d = x.shape
    assert d == D_MODEL and seq % STEP == 0 and STEP >= HIST
    assert mem.shape == (batch, N_MEM, D_MODEL)
    w_in_bf = w_in.astype(bf16)
    w_out_bf = w_out.astype(bf16)
    kvm = _kv_mem_all(mem.reshape(batch * N_MEM, D_MODEL), w_mem_kv.astype(bf16))
    kvm = kvm.reshape(DEPTH, batch, N_MEM, 2 * E_MEM)
    tshift = jnp.pad(rel_bias[:, :, 1:], ((0, 0), (0, 0), (0, BASE_W - (N_REL - 1))))
    for layer in range(DEPTH):
        g = ln_g[layer].reshape(1, D_MODEL)
        b = ln_b[layer].reshape(1, D_MODEL)
        w_l = w_in_bf[layer]
        if layer % 2 == 0:
            w_tok = jnp.concatenate([w_l[:, _K0:_K0 + E_MIX], w_l[:, _QM0:]], axis=1)
            w_qvt = jnp.concatenate([w_l[:, _Q0:_Q0 + E_MIX], w_l[:, _V0:_V0 + E_MIX]], axis=1).T
            x = _layer_call("attn", x, [w_tok, w_qvt], w_out_bf[layer], kvm[layer],
                            _tile_bias(tshift[layer // 2]), g, b)
        else:
            x = _layer_call("conv", x, [w_l], w_out_bf[layer], kvm[layer],
                            conv_w[layer // 2], g, b)
    return x
```

```python
import functools
import math

import jax
import jax.numpy as jnp
from jax import lax
from jax.experimental import pallas as pl
from jax.experimental.pallas import tpu as pltpu

D_MODEL = 1024
DEPTH = 4
CHUNK = 64
N_PREV = 8
N_HEADS = 16
HEAD_DIM = 64
E_MIX = N_HEADS * HEAD_DIM
REL_CLIP = 128
N_REL = 2 * REL_CLIP + 1
CONV_W = 3
N_MEM = 256
MEM_HEADS = 4
MEM_HEAD_DIM = 128
E_MEM = MEM_HEADS * MEM_HEAD_DIM
E_BRANCH = E_MIX + E_MEM
DN_ALPHA = (2.0 * DEPTH) ** 0.25
LN_EPS = 1e-5

LANES = 128
TILE = 256
STEP_TILES = 2
STEP = STEP_TILES * TILE
HIST = N_PREV * CHUNK
WIN = HIST + TILE
KV_ROWS = HIST + STEP
N_PAIRS = E_MIX // LANES
HEADS_PER_PAIR = LANES // HEAD_DIM
ONES_ROWS = 16
VT_ROWS = LANES + 2 * ONES_ROWS
QK_ROWS = 128
K_LANES = 2 * LANES
HALF = TILE // 2
HALF_WIN = HIST + HALF
BASE_W = 384
NEG = -1e30
LOG2E = math.log2(math.e)
VMEM_BYTES = 64 * 1024 * 1024
VMEM_TEMP_BYTES = 8 * 1024 * 1024
VMEM_RESERVE_BYTES = 4 * 1024 * 1024

_Q0, _K0, _V0 = 0, E_MIX, 2 * E_MIX
_QM0 = 3 * E_MIX
_Z0 = 3 * E_MIX + E_MEM
_AK0, _AQM0, _AZ0 = 0, E_MIX, E_MIX + E_MEM

bf16 = jnp.bfloat16
f32 = jnp.float32


def _nbytes(shape, dtype):
    return math.prod(shape) * jnp.dtype(dtype).itemsize


def _vmem_limit(windows, scratch=()):
    need = (sum(n * _nbytes(s, d) for s, d, n in windows)
            + sum(_nbytes(s, d) for s, d in scratch) + VMEM_TEMP_BYTES)
    limit = VMEM_BYTES - VMEM_RESERVE_BYTES
    assert need <= limit, need
    return limit


def _dot(a, b):
    return jnp.dot(a, b, preferred_element_type=f32)


def _dot_nt(a, b):
    return lax.dot_general(a, b, (((1,), (1,)), ((), ())), preferred_element_type=f32)


def _softmax2_pv(s, v):
    m = jnp.max(s, axis=-1, keepdims=True)
    p = jnp.exp2(s - m)
    l = jnp.sum(p, axis=-1, keepdims=True)
    return _dot(p.astype(bf16), v) / l


def _kvmem_kernel(mem_ref, w_ref, o_ref):
    o_ref[...] = _dot(mem_ref[...].astype(bf16), w_ref[...]).astype(bf16)


def _kv_mem_all(mem2d, w_mem_kv_bf):
    rows = mem2d.shape[0]
    return pl.pallas_call(
        _kvmem_kernel,
        grid=(DEPTH,),
        in_specs=[
            pl.BlockSpec((rows, D_MODEL), lambda l: (0, 0)),
            pl.BlockSpec((None, D_MODEL, 2 * E_MEM), lambda l: (l, 0, 0)),
        ],
        out_specs=pl.BlockSpec((None, rows, 2 * E_MEM), lambda l: (l, 0, 0)),
        out_shape=jax.ShapeDtypeStruct((DEPTH, rows, 2 * E_MEM), bf16),
        compiler_params=pltpu.CompilerParams(
            dimension_semantics=("arbitrary",),
            vmem_limit_bytes=_vmem_limit([((rows, D_MODEL), f32, 2),
                                          ((D_MODEL, 2 * E_MEM), bf16, 2),
                                          ((rows, 2 * E_MEM), bf16, 2)])),
        name="kv_mem_proj",
    )(mem2d, w_mem_kv_bf)


_FAR_ROWS = HIST - REL_CLIP


def _bias_kernel(tshift_ref, o_ref):
    c = lax.broadcasted_iota(jnp.int32, (N_HEADS, BASE_W), 1)
    ts = tshift_ref[...]
    last = ts[:, 2 * REL_CLIP - 1:2 * REL_CLIP]
    base = jnp.where(c >= 2 * REL_CLIP, last, ts) * LOG2E
    far = base[:, BASE_W - HALF:]

    qchunk = lax.broadcasted_iota(jnp.int32, (N_HEADS, HALF), 1) // CHUNK
    d_far = (lax.broadcasted_iota(jnp.int32, (N_HEADS, _FAR_ROWS, HALF), 1) // CHUNK
             - lax.broadcasted_iota(jnp.int32, (N_HEADS, _FAR_ROWS, HALF), 2) // CHUNK)
    o_ref[:, 0:_FAR_ROWS, :] = jnp.where(d_far >= 0, far[:, None, :], NEG)

    def row(p, carry):
        rolled = pltpu.roll(base, (p + (BASE_W - HALF_WIN + 1)) % BASE_W, 1)[:, 0:HALF]
        val = jnp.where(p // CHUNK - qchunk <= N_PREV, rolled, NEG)
        o_ref[:, pl.ds(p, 1), :] = val[:, None, :]
        return carry

    lax.fori_loop(_FAR_ROWS, HALF_WIN, row, 0, unroll=8)


def _tile_bias(tshift):
    return pl.pallas_call(
        _bias_kernel,
        out_shape=jax.ShapeDtypeStruct((N_HEADS, HALF_WIN, HALF), f32),
        compiler_params=pltpu.CompilerParams(
            vmem_limit_bytes=_vmem_limit([((N_HEADS, BASE_W), f32, 1),
                                          ((N_HEADS, HALF_WIN, HALF), f32, 1)])),
        name="rel_bias_tile",
    )(tshift)


def _init_residual(u_scr):
    @pl.when(pl.program_id(0) == 0)
    def _():
        u_scr[...] = jnp.zeros(u_scr.shape, f32)


def _norm_previous_step(u_scr, g_ref, b_ref, o_ref):
    u = u_scr[...]
    mu = jnp.mean(u, axis=-1, keepdims=True)
    uc = u - mu
    var = jnp.mean(uc * uc, axis=-1, keepdims=True)
    o_ref[...] = uc * lax.rsqrt(var + LN_EPS) * g_ref[...] + b_ref[...]


def _layer_tail(xf, xb, mix, w_qm, w_z, w_out_ref, kvm_ref, u_scr, r0):
    qm = _dot(xb, w_qm) * (LOG2E / math.sqrt(MEM_HEAD_DIM))
    mem_outs = []
    for h in range(MEM_HEADS):
        lo = h * MEM_HEAD_DIM
        qh = qm[:, lo:lo + MEM_HEAD_DIM].astype(bf16)
        kh = kvm_ref[:, lo:lo + MEM_HEAD_DIM]
        vh = kvm_ref[:, E_MEM + lo:E_MEM + lo + MEM_HEAD_DIM]
        mem_outs.append(_softmax2_pv(_dot_nt(qh, kh), vh))
    z = _dot(xb, w_z)
    gate = z * jax.nn.sigmoid(z)
    y = jnp.concatenate([mix] + mem_outs, axis=-1) * gate
    out = _dot(y.astype(bf16), w_out_ref[...])
    u_scr[r0:r0 + TILE, :] = DN_ALPHA * xf + out


def _attn_core(base, qt_scr, k_scr, vt_scr, bias_ref, s_scr, p_scr, mixt_scr):
    zeros = jnp.zeros((HEAD_DIM, TILE), bf16)
    flag_rows = jnp.where(lax.broadcasted_iota(jnp.int32, (LANES, TILE), 0) == 0, NEG, 0.0).astype(bf16)

    def scores(hp, e):
        lo = e * HEAD_DIM
        qth = qt_scr[hp, lo:lo + HEAD_DIM, :]
        qtm = jnp.concatenate([qth, zeros, flag_rows] if e == 0 else [zeros, qth, flag_rows], axis=0)
        h = HEADS_PER_PAIR * hp + e
        ms = [None, None]
        for a, b in [(r, r + QK_ROWS) for r in range(0, WIN, QK_ROWS)]:
            halves = [half for half in range(2)
                      if max(a, half * HALF) < min(b, half * HALF + HALF_WIN)]
            c0 = halves[0] * HALF
            st = _dot(k_scr[hp, base + a:base + b, :], qtm[:, c0:(halves[-1] + 1) * HALF])
            for half in halves:
                ra, rb = max(a, half * HALF), min(b, half * HALF + HALF_WIN)
                cols = slice(half * HALF, (half + 1) * HALF)
                sh = (st[ra - a:rb - a, half * HALF - c0:(half + 1) * HALF - c0]
                      + bias_ref[h, ra - half * HALF:rb - half * HALF, :])
                s_scr[e, ra:rb, cols] = sh
                mh = jnp.max(sh, axis=0, keepdims=True)
                ms[half] = mh if ms[half] is None else jnp.maximum(ms[half], mh)
        return ms

    def exponentials(e, ms):
        for half in range(2):
            ra, rb = half * HALF, half * HALF + HALF_WIN
            cols = slice(half * HALF, (half + 1) * HALF)
            p_scr[e, ra:rb, cols] = jnp.exp2(s_scr[e, ra:rb, cols] - ms[half]).astype(bf16)

    def weighted_values(hp, e):
        lo = e * HEAD_DIM
        r0 = e * (HEAD_DIM + ONES_ROWS)
        ot = _dot(vt_scr[hp, r0:r0 + HEAD_DIM + ONES_ROWS, base:base + WIN], p_scr[e])
        if e == 0:
            l, num = ot[0:1, :], ot[ONES_ROWS:, :]
        else:
            l, num = ot[HEAD_DIM:HEAD_DIM + 1, :], ot[0:HEAD_DIM, :]
        mixt_scr[hp, lo:lo + HEAD_DIM, :] = num / l

    def pair_body(hp, m1):
        weighted_values(hp, 0)
        m0_next = scores(hp + 1, 0)
        exponentials(1, m1)
        weighted_values(hp, 1)
        m1_next = scores(hp + 1, 1)
        exponentials(0, m0_next)
        return m1_next

    m0 = scores(0, 0)
    m1 = scores(0, 1)
    exponentials(0, m0)
    for hp in range(N_PAIRS - 1):
        m1 = pair_body(hp, m1)
    weighted_values(N_PAIRS - 1, 0)
    exponentials(1, m1)
    weighted_values(N_PAIRS - 1, 1)


def _attn_layer_kernel(x_ref, w_tok_ref, w_qvt_ref, w_out_ref, kvm_ref, bias_ref, g_ref, b_ref,
                       o_ref, u_scr, qt_scr, k_scr, vt_scr, s_scr, p_scr, mixt_scr, *,
                       n_steps, n_blocks):
    si = lax.rem(pl.program_id(0), n_steps)
    _init_residual(u_scr)

    @pl.when(pl.program_id(0) == n_blocks)
    def _():
        _norm_previous_step(u_scr, g_ref, b_ref, o_ref)

    @pl.when(pl.program_id(0) < n_blocks)
    def _():
        @pl.when(pl.program_id(0) == 0)
        def _():
            k_scr[:, :, LANES:K_LANES] = jnp.zeros((N_PAIRS, KV_ROWS, LANES), bf16)
            p_scr[:, HALF_WIN:WIN, 0:HALF] = jnp.zeros((HEADS_PER_PAIR, WIN - HALF_WIN, HALF), bf16)
            p_scr[:, 0:HALF, HALF:TILE] = jnp.zeros((HEADS_PER_PAIR, HALF, HALF), bf16)

        @pl.when(si == 0)
        def _():
            lane = lax.broadcasted_iota(jnp.int32, (N_PAIRS, HIST, K_LANES), 2)
            k_scr[:, 0:HIST, :] = jnp.where(lane == LANES, 1.0, 0.0).astype(bf16)
            vt_scr[:, :, 0:HIST] = jnp.zeros((N_PAIRS, VT_ROWS, HIST), bf16)

        @pl.when(si > 0)
        def _():
            k_scr[:, 0:HIST, :] = k_scr[:, STEP:KV_ROWS, :]
            vt_scr[:, :, 0:HIST] = vt_scr[:, :, STEP:KV_ROWS]

        _norm_previous_step(u_scr, g_ref, b_ref, o_ref)
        core = functools.partial(_attn_core, qt_scr=qt_scr, k_scr=k_scr, vt_scr=vt_scr,
                                 bias_ref=bias_ref, s_scr=s_scr, p_scr=p_scr, mixt_scr=mixt_scr)
        for sub in range(STEP_TILES):
            r0 = sub * TILE
            xf = x_ref[r0:r0 + TILE, :]
            xb = xf.astype(bf16)
            qt = _dot_nt(w_qvt_ref[0:E_MIX, :], xb) * (LOG2E / math.sqrt(HEAD_DIM))
            qt_scr[...] = qt.astype(bf16).reshape(N_PAIRS, LANES, TILE)
            new = slice(HIST + r0, WIN + r0)
            vt = _dot_nt(w_qvt_ref[E_MIX:2 * E_MIX, :], xb)
            vt_scr[:, ONES_ROWS:ONES_ROWS + LANES, new] = vt.astype(bf16).reshape(N_PAIRS, LANES, TILE)
            ones = jnp.ones((N_PAIRS, ONES_ROWS, TILE), bf16)
            vt_scr[:, 0:ONES_ROWS, new] = ones
            vt_scr[:, ONES_ROWS + LANES:VT_ROWS, new] = ones
            k = _dot(xb, w_tok_ref[:, _AK0:_AK0 + E_MIX])
            for hp in range(N_PAIRS):
                k_scr[hp, new, 0:LANES] = k[:, hp * LANES:(hp + 1) * LANES].astype(bf16)

            core(r0)
            mix = jnp.concatenate([mixt_scr[hp].T for hp in range(N_PAIRS)], axis=-1)
            _layer_tail(xf, xb, mix, w_tok_ref[:, _AQM0:_AQM0 + E_MEM],
                        w_tok_ref[:, _AZ0:_AZ0 + E_BRANCH], w_out_ref, kvm_ref, u_scr, r0)


_CARRY = 8


def _conv_layer_kernel(x_ref, w_in_ref, w_out_ref, kvm_ref, cw_ref, g_ref, b_ref, o_ref,
                       u_scr, cu_scr, *, n_steps, n_blocks):
    si = lax.rem(pl.program_id(0), n_steps)
    _init_residual(u_scr)

    @pl.when(pl.program_id(0) == n_blocks)
    def _():
        _norm_previous_step(u_scr, g_ref, b_ref, o_ref)

    @pl.when(pl.program_id(0) < n_blocks)
    def _():
        @pl.when(si == 0)
        def _():
            cu_scr[0:_CARRY, :] = jnp.zeros((_CARRY, E_MIX), f32)

        @pl.when(si > 0)
        def _():
            cu_scr[0:_CARRY, :] = cu_scr[STEP:STEP + _CARRY, :]

        _norm_previous_step(u_scr, g_ref, b_ref, o_ref)
        cw = cw_ref[...]
        for sub in range(STEP_TILES):
            r0 = sub * TILE
            xf = x_ref[r0:r0 + TILE, :]
            xb = xf.astype(bf16)
            p0 = _dot(xb, w_in_ref[:, _Q0:_Q0 + E_MIX])
            p1 = _dot(xb, w_in_ref[:, _K0:_K0 + E_MIX])
            p2 = _dot(xb, w_in_ref[:, _V0:_V0 + E_MIX])
            c0 = _CARRY + r0
            cu_scr[c0:c0 + TILE, :] = p1 * p2
            conv = cu_scr[c0:c0 + TILE, :] * cw[CONV_W - 1:CONV_W, :]
            for t in range(1, CONV_W):
                conv = conv + cu_scr[c0 - t:c0 - t + TILE, :] * cw[CONV_W - 1 - t:CONV_W - t, :]
            mix = p0 * conv
            _layer_tail(xf, xb, mix, w_in_ref[:, _QM0:_QM0 + E_MEM], w_in_ref[:, _Z0:_Z0 + E_BRANCH],
                        w_out_ref, kvm_ref, u_scr, r0)


def _layer_call(kind, x, weights, w_out_l, kvm_l, extra, g, b):
    batch, seq, _ = x.shape
    n_steps = seq // STEP
    last = batch * n_steps - 1

    def in_block(g):
        t = jnp.minimum(g, last)
        return t // n_steps, t % n_steps

    def out_block(g):
        t = jnp.maximum(g - 1, 0)
        return t // n_steps, t % n_steps

    def const(a):
        return a.shape, 1, lambda g: (0,) * a.ndim

    args = (x, *weights, w_out_l, kvm_l, extra, g, b)
    operands = ([((None, STEP, D_MODEL), 2, lambda g: (*in_block(g), 0))]
                + [const(w) for w in weights]
                + [const(w_out_l), ((None, N_MEM, 2 * E_MEM), 2, lambda g: (in_block(g)[0], 0, 0)),
                   const(extra), const(g), const(b)])
    if kind == "attn":
        body = _attn_layer_kernel
        scratch = [
            ((STEP, D_MODEL), f32),
            ((N_PAIRS, LANES, TILE), bf16),
            ((N_PAIRS, KV_ROWS, K_LANES), bf16),
            ((N_PAIRS, VT_ROWS, KV_ROWS), bf16),
            ((HEADS_PER_PAIR, WIN, TILE), f32),
            ((HEADS_PER_PAIR, WIN, TILE), bf16),
            ((N_PAIRS, LANES, TILE), f32),
        ]
    else:
        body = _conv_layer_kernel
        scratch = [((STEP, D_MODEL), f32), ((STEP + _CARRY, E_MIX), f32)]
    windows = [(tuple(d for d in shape if d is not None), a.dtype, n)
               for (shape, n, _), a in zip(operands, args)]
    windows.append(((STEP, D_MODEL), f32, 2))
    return pl.pallas_call(
        functools.partial(body, n_steps=n_steps, n_blocks=batch * n_steps),
        grid=(batch * n_steps + 1,),
        in_specs=[pl.BlockSpec(shape, imap, pipeline_mode=pl.Buffered(1) if n == 1 else None)
                  for shape, n, imap in operands],
        out_specs=pl.BlockSpec((None, STEP, D_MODEL), lambda g: (*out_block(g), 0)),
        out_shape=jax.ShapeDtypeStruct(x.shape, f32),
        scratch_shapes=[pltpu.VMEM(s, d) for s, d in scratch],
        compiler_params=pltpu.CompilerParams(
            dimension_semantics=("arbitrary",), vmem_limit_bytes=_vmem_limit(windows, scratch)),
        name=f"{kind}_layer",
    )(*args)


def kernel(x, mem, w_in, w_mem_kv, w_out, rel_bias, conv_w, ln_g, ln_b):
    batch, seq, d = x.shape
    assert d == D_MODEL and seq % STEP == 0 and STEP >= HIST
    assert mem.shape == (batch, N_MEM, D_MODEL)
    w_in_bf = w_in.astype(bf16)
    w_out_bf = w_out.astype(bf16)
    kvm = _kv_mem_all(mem.reshape(batch * N_MEM, D_MODEL), w_mem_kv.astype(bf16))
    kvm = kvm.reshape(DEPTH, batch, N_MEM, 2 * E_MEM)
    tshift = jnp.pad(rel_bias[:, :, 1:], ((0, 0), (0, 0), (0, BASE_W - (N_REL - 1))))
    for layer in range(DEPTH):
        g = ln_g[layer].reshape(1, D_MODEL)
        b = ln_b[layer].reshape(1, D_MODEL)
        w_l = w_in_bf[layer]
        if layer % 2 == 0:
            w_tok = jnp.concatenate([w_l[:, _K0:_K0 + E_MIX], w_l[:, _QM0:]], axis=1)
            w_qvt = jnp.concatenate([w_l[:, _Q0:_Q0 + E_MIX], w_l[:, _V0:_V0 + E_MIX]], axis=1).T
            x = _layer_call("attn", x, [w_tok, w_qvt], w_out_bf[layer], kvm[layer],
                            _tile_bias(tshift[layer // 2]), g, b)
        else:
            x = _layer_call("conv", x, [w_l], w_out_bf[layer], kvm[layer],
                            conv_w[layer // 2], g, b)
    return x
```

```python
import functools
import math

import jax
import jax.numpy as jnp
from jax import lax
from jax.experimental import pallas as pl
from jax.experimental.pallas import tpu as pltpu

D_MODEL = 1024
DEPTH = 4
CHUNK = 64
N_PREV = 8
N_HEADS = 16
HEAD_DIM = 64
E_MIX = N_HEADS * HEAD_DIM
REL_CLIP = 128
N_REL = 2 * REL_CLIP + 1
CONV_W = 3
N_MEM = 256
MEM_HEADS = 4
MEM_HEAD_DIM = 128
E_MEM = MEM_HEADS * MEM_HEAD_DIM
E_BRANCH = E_MIX + E_MEM
DN_ALPHA = (2.0 * DEPTH) ** 0.25
LN_EPS = 1e-5

LANES = 128
TILE = 256
STEP_TILES = 2
STEP = STEP_TILES * TILE
HIST = N_PREV * CHUNK
WIN = HIST + TILE
KV_ROWS = HIST + STEP
N_PAIRS = E_MIX // LANES
HEADS_PER_PAIR = LANES // HEAD_DIM
ONES_ROWS = 16
VT_ROWS = LANES + 2 * ONES_ROWS
LN_ROWS = 32
QK_ROWS = 128
K_LANES = 2 * LANES
HALF = TILE // 2
HALF_WIN = HIST + HALF
BASE_W = 384
NEG = -1e30
LOG2E = math.log2(math.e)
VMEM_BYTES = 64 * 1024 * 1024
VMEM_TEMP_BYTES = 8 * 1024 * 1024
VMEM_RESERVE_BYTES = 4 * 1024 * 1024

_Q0, _K0, _V0 = 0, E_MIX, 2 * E_MIX
_QM0 = 3 * E_MIX
_Z0 = 3 * E_MIX + E_MEM
_AK0, _AQM0, _AZ0 = _K0, _QM0, _Z0

bf16 = jnp.bfloat16
f32 = jnp.float32


def _nbytes(shape, dtype):
    return math.prod(shape) * jnp.dtype(dtype).itemsize


def _vmem_limit(windows, scratch=()):
    need = (sum(n * _nbytes(s, d) for s, d, n in windows)
            + sum(_nbytes(s, d) for s, d in scratch) + VMEM_TEMP_BYTES)
    limit = VMEM_BYTES - VMEM_RESERVE_BYTES
    assert need <= limit, need
    return limit


def _dot(a, b):
    return jnp.dot(a, b, preferred_element_type=f32)


def _dot_nt(a, b):
    return lax.dot_general(a, b, (((1,), (1,)), ((), ())), preferred_element_type=f32)


def _softmax2_pv(s, v):
    m = jnp.max(s, axis=-1, keepdims=True)
    p = jnp.exp2(s - m)
    l = jnp.sum(p, axis=-1, keepdims=True)
    return _dot(p.astype(bf16), v) / l


def _kvmem_kernel(mem_ref, w_ref, o_ref):
    o_ref[...] = _dot(mem_ref[...].astype(bf16), w_ref[...]).astype(bf16)


def _kv_mem_all(mem2d, w_mem_kv_bf):
    rows = mem2d.shape[0]
    return pl.pallas_call(
        _kvmem_kernel,
        grid=(DEPTH,),
        in_specs=[
            pl.BlockSpec((rows, D_MODEL), lambda l: (0, 0)),
            pl.BlockSpec((None, D_MODEL, 2 * E_MEM), lambda l: (l, 0, 0)),
        ],
        out_specs=pl.BlockSpec((None, rows, 2 * E_MEM), lambda l: (l, 0, 0)),
        out_shape=jax.ShapeDtypeStruct((DEPTH, rows, 2 * E_MEM), bf16),
        compiler_params=pltpu.CompilerParams(
            dimension_semantics=("arbitrary",),
            vmem_limit_bytes=_vmem_limit([((rows, D_MODEL), f32, 2),
                                          ((D_MODEL, 2 * E_MEM), bf16, 2),
                                          ((rows, 2 * E_MEM), bf16, 2)])),
        name="kv_mem_proj",
    )(mem2d, w_mem_kv_bf)


_FAR_ROWS = HIST - REL_CLIP


def _bias_kernel(tshift_ref, o_ref):
    c = lax.broadcasted_iota(jnp.int32, (N_HEADS, BASE_W), 1)
    ts = tshift_ref[...]
    last = ts[:, 2 * REL_CLIP - 1:2 * REL_CLIP]
    base = jnp.where(c >= 2 * REL_CLIP, last, ts) * LOG2E
    far = base[:, BASE_W - HALF:]

    qchunk = lax.broadcasted_iota(jnp.int32, (N_HEADS, HALF), 1) // CHUNK
    d_far = (lax.broadcasted_iota(jnp.int32, (N_HEADS, _FAR_ROWS, HALF), 1) // CHUNK
             - lax.broadcasted_iota(jnp.int32, (N_HEADS, _FAR_ROWS, HALF), 2) // CHUNK)
    o_ref[:, 0:_FAR_ROWS, :] = jnp.where(d_far >= 0, far[:, None, :], NEG)

    def row(p, carry):
        rolled = pltpu.roll(base, (p + (BASE_W - HALF_WIN + 1)) % BASE_W, 1)[:, 0:HALF]
        val = jnp.where(p // CHUNK - qchunk <= N_PREV, rolled, NEG)
        o_ref[:, pl.ds(p, 1), :] = val[:, None, :]
        return carry

    lax.fori_loop(_FAR_ROWS, HALF_WIN, row, 0, unroll=8)


def _tile_bias(tshift):
    return pl.pallas_call(
        _bias_kernel,
        out_shape=jax.ShapeDtypeStruct((N_HEADS, HALF_WIN, HALF), f32),
        compiler_params=pltpu.CompilerParams(
            vmem_limit_bytes=_vmem_limit([((N_HEADS, BASE_W), f32, 1),
                                          ((N_HEADS, HALF_WIN, HALF), f32, 1)])),
        name="rel_bias_tile",
    )(tshift)


def _init_residual(u_scr):
    @pl.when(pl.program_id(0) == 0)
    def _():
        u_scr[...] = jnp.zeros(u_scr.shape, f32)


def _norm_previous_step(u_scr, g_ref, b_ref, o_ref):
    gain, shift = g_ref[...], b_ref[...]
    for r in range(0, STEP, LN_ROWS):
        u = u_scr[r:r + LN_ROWS, :]
        mu = jnp.mean(u, axis=-1, keepdims=True)
        uc = u - mu
        var = jnp.mean(uc * uc, axis=-1, keepdims=True)
        o_ref[r:r + LN_ROWS, :] = uc * lax.rsqrt(var + LN_EPS) * gain + shift


def _layer_tail(xf, xb, mix, w_qm, w_z, w_out_ref, kvm_ref, u_scr, r0):
    qm = _dot(xb, w_qm) * (LOG2E / math.sqrt(MEM_HEAD_DIM))
    mem_outs = []
    for h in range(MEM_HEADS):
        lo = h * MEM_HEAD_DIM
        qh = qm[:, lo:lo + MEM_HEAD_DIM].astype(bf16)
        kh = kvm_ref[:, lo:lo + MEM_HEAD_DIM]
        vh = kvm_ref[:, E_MEM + lo:E_MEM + lo + MEM_HEAD_DIM]
        mem_outs.append(_softmax2_pv(_dot_nt(qh, kh), vh))
    z = _dot(xb, w_z)
    gate = z * jax.nn.sigmoid(z)
    y = jnp.concatenate([mix] + mem_outs, axis=-1) * gate
    out = _dot(y.astype(bf16), w_out_ref[...])
    u_scr[r0:r0 + TILE, :] = DN_ALPHA * xf + out


def _attn_core(base, qt_scr, k_scr, vt_scr, bias_ref, s_scr, p_scr, mixt_scr):
    zeros = jnp.zeros((HEAD_DIM, TILE), bf16)
    flag_rows = jnp.where(lax.broadcasted_iota(jnp.int32, (LANES, TILE), 0) == 0, NEG, 0.0).astype(bf16)

    def scores(hp, e):
        lo = e * HEAD_DIM
        qth = qt_scr[hp, lo:lo + HEAD_DIM, :]
        qtm = jnp.concatenate([qth, zeros, flag_rows] if e == 0 else [zeros, qth, flag_rows], axis=0)
        h = HEADS_PER_PAIR * hp + e
        ms = [None, None]
        for a, b in [(r, r + QK_ROWS) for r in range(0, WIN, QK_ROWS)]:
            halves = [half for half in range(2)
                      if max(a, half * HALF) < min(b, half * HALF + HALF_WIN)]
            c0 = halves[0] * HALF
            st = _dot(k_scr[hp, base + a:base + b, :], qtm[:, c0:(halves[-1] + 1) * HALF])
            for half in halves:
                ra, rb = max(a, half * HALF), min(b, half * HALF + HALF_WIN)
                cols = slice(half * HALF, (half + 1) * HALF)
                sh = (st[ra - a:rb - a, half * HALF - c0:(half + 1) * HALF - c0]
                      + bias_ref[h, ra - half * HALF:rb - half * HALF, :])
                s_scr[e, ra:rb, cols] = sh
                mh = jnp.max(sh, axis=0, keepdims=True)
                ms[half] = mh if ms[half] is None else jnp.maximum(ms[half], mh)
        return ms

    def exponentials(e, ms):
        for half in range(2):
            ra, rb = half * HALF, half * HALF + HALF_WIN
            cols = slice(half * HALF, (half + 1) * HALF)
            p_scr[e, ra:rb, cols] = jnp.exp2(s_scr[e, ra:rb, cols] - ms[half]).astype(bf16)

    def weighted_values(hp, e):
        lo = e * HEAD_DIM
        r0 = e * (HEAD_DIM + ONES_ROWS)
        ot = _dot(vt_scr[hp, r0:r0 + HEAD_DIM + ONES_ROWS, base:base + WIN], p_scr[e])
        if e == 0:
            l, num = ot[0:1, :], ot[ONES_ROWS:, :]
        else:
            l, num = ot[HEAD_DIM:HEAD_DIM + 1, :], ot[0:HEAD_DIM, :]
        mixt_scr[hp, lo:lo + HEAD_DIM, :] = num / l

    def pair_body(hp, m1):
        weighted_values(hp, 0)
        m0_next = scores(hp + 1, 0)
        exponentials(1, m1)
        weighted_values(hp, 1)
        m1_next = scores(hp + 1, 1)
        exponentials(0, m0_next)
        return m1_next

    m0 = scores(0, 0)
    m1 = scores(0, 1)
    exponentials(0, m0)
    for hp in range(N_PAIRS - 1):
        m1 = pair_body(hp, m1)
    weighted_values(N_PAIRS - 1, 0)
    exponentials(1, m1)
    weighted_values(N_PAIRS - 1, 1)


def _attn_layer_kernel(x_ref, w_tok_ref, w_qvt_ref, w_out_ref, kvm_ref, bias_ref, g_ref, b_ref,
                       o_ref, u_scr, qt_scr, k_scr, vt_scr, s_scr, p_scr, mixt_scr, *,
                       n_steps, n_blocks):
    si = lax.rem(pl.program_id(0), n_steps)
    _init_residual(u_scr)

    @pl.when(pl.program_id(0) == n_blocks)
    def _():
        _norm_previous_step(u_scr, g_ref, b_ref, o_ref)

    @pl.when(pl.program_id(0) < n_blocks)
    def _():
        @pl.when(pl.program_id(0) == 0)
        def _():
            k_scr[:, :, LANES:K_LANES] = jnp.zeros((N_PAIRS, KV_ROWS, LANES), bf16)
            p_scr[:, HALF_WIN:WIN, 0:HALF] = jnp.zeros((HEADS_PER_PAIR, WIN - HALF_WIN, HALF), bf16)
            p_scr[:, 0:HALF, HALF:TILE] = jnp.zeros((HEADS_PER_PAIR, HALF, HALF), bf16)

        @pl.when(si == 0)
        def _():
            lane = lax.broadcasted_iota(jnp.int32, (N_PAIRS, HIST, K_LANES), 2)
            k_scr[:, 0:HIST, :] = jnp.where(lane == LANES, 1.0, 0.0).astype(bf16)
            vt_scr[:, :, 0:HIST] = jnp.zeros((N_PAIRS, VT_ROWS, HIST), bf16)

        @pl.when(si > 0)
        def _():
            k_scr[:, 0:HIST, :] = k_scr[:, STEP:KV_ROWS, :]
            vt_scr[:, :, 0:HIST] = vt_scr[:, :, STEP:KV_ROWS]

        _norm_previous_step(u_scr, g_ref, b_ref, o_ref)
        core = functools.partial(_attn_core, qt_scr=qt_scr, k_scr=k_scr, vt_scr=vt_scr,
                                 bias_ref=bias_ref, s_scr=s_scr, p_scr=p_scr, mixt_scr=mixt_scr)
        for sub in range(STEP_TILES):
            r0 = sub * TILE
            xf = x_ref[r0:r0 + TILE, :]
            xb = xf.astype(bf16)
            qt = _dot_nt(w_qvt_ref[0:E_MIX, :], xb) * (LOG2E / math.sqrt(HEAD_DIM))
            qt_scr[...] = qt.astype(bf16).reshape(N_PAIRS, LANES, TILE)
            new = slice(HIST + r0, WIN + r0)
            vt = _dot_nt(w_qvt_ref[E_MIX:2 * E_MIX, :], xb)
            vt_scr[:, ONES_ROWS:ONES_ROWS + LANES, new] = vt.astype(bf16).reshape(N_PAIRS, LANES, TILE)
            ones = jnp.ones((N_PAIRS, ONES_ROWS, TILE), bf16)
            vt_scr[:, 0:ONES_ROWS, new] = ones
            vt_scr[:, ONES_ROWS + LANES:VT_ROWS, new] = ones
            k = _dot(xb, w_tok_ref[:, _AK0:_AK0 + E_MIX])
            for hp in range(N_PAIRS):
                k_scr[hp, new, 0:LANES] = k[:, hp * LANES:(hp + 1) * LANES].astype(bf16)

            core(r0)
            mix = jnp.concatenate([mixt_scr[hp].T for hp in range(N_PAIRS)], axis=-1)
            _layer_tail(xf, xb, mix, w_tok_ref[:, _AQM0:_AQM0 + E_MEM],
                        w_tok_ref[:, _AZ0:_AZ0 + E_BRANCH], w_out_ref, kvm_ref, u_scr, r0)


_CARRY = 8


def _conv_layer_kernel(x_ref, w_in_ref, w_out_ref, kvm_ref, cw_ref, g_ref, b_ref, o_ref,
                       u_scr, cu_scr, *, n_steps, n_blocks):
    si = lax.rem(pl.program_id(0), n_steps)
    _init_residual(u_scr)

    @pl.when(pl.program_id(0) == n_blocks)
    def _():
        _norm_previous_step(u_scr, g_ref, b_ref, o_ref)

    @pl.when(pl.program_id(0) < n_blocks)
    def _():
        @pl.when(si == 0)
        def _():
            cu_scr[0:_CARRY, :] = jnp.zeros((_CARRY, E_MIX), f32)

        @pl.when(si > 0)
        def _():
            cu_scr[0:_CARRY, :] = cu_scr[STEP:STEP + _CARRY, :]

        _norm_previous_step(u_scr, g_ref, b_ref, o_ref)
        cw = cw_ref[...]
        for sub in range(STEP_TILES):
            r0 = sub * TILE
            xf = x_ref[r0:r0 + TILE, :]
            xb = xf.astype(bf16)
            p0 = _dot(xb, w_in_ref[:, _Q0:_Q0 + E_MIX])
            p1 = _dot(xb, w_in_ref[:, _K0:_K0 + E_MIX])
            p2 = _dot(xb, w_in_ref[:, _V0:_V0 + E_MIX])
            c0 = _CARRY + r0
            cu_scr[c0:c0 + TILE, :] = p1 * p2
            conv = cu_scr[c0:c0 + TILE, :] * cw[CONV_W - 1:CONV_W, :]
            for t in range(1, CONV_W):
                conv = conv + cu_scr[c0 - t:c0 - t + TILE, :] * cw[CONV_W - 1 - t:CONV_W - t, :]
            mix = p0 * conv
            _layer_tail(xf, xb, mix, w_in_ref[:, _QM0:_QM0 + E_MEM], w_in_ref[:, _Z0:_Z0 + E_BRANCH],
                        w_out_ref, kvm_ref, u_scr, r0)


def _layer_call(kind, x, weights, w_out_l, kvm_l, extra, g, b):
    batch, seq, _ = x.shape
    n_steps = seq // STEP
    last = batch * n_steps - 1

    def in_block(g):
        t = jnp.minimum(g, last)
        return t // n_steps, t % n_steps

    def out_block(g):
        t = jnp.maximum(g - 1, 0)
        return t // n_steps, t % n_steps

    def const(a):
        return a.shape, 1, lambda g: (0,) * a.ndim

    args = (x, *weights, w_out_l, kvm_l, extra, g, b)
    operands = ([((None, STEP, D_MODEL), 2, lambda g: (*in_block(g), 0))]
                + [const(w) for w in weights]
                + [const(w_out_l), ((None, N_MEM, 2 * E_MEM), 2, lambda g: (in_block(g)[0], 0, 0)),
                   const(extra), const(g), const(b)])
    if kind == "attn":
        body = _attn_layer_kernel
        scratch = [
            ((STEP, D_MODEL), f32),
            ((N_PAIRS, LANES, TILE), bf16),
            ((N_PAIRS, KV_ROWS, K_LANES), bf16),
            ((N_PAIRS, VT_ROWS, KV_ROWS), bf16),
            ((HEADS_PER_PAIR, WIN, TILE), f32),
            ((HEADS_PER_PAIR, WIN, TILE), bf16),
            ((N_PAIRS, LANES, TILE), f32),
        ]
    else:
        body = _conv_layer_kernel
        scratch = [((STEP, D_MODEL), f32), ((STEP + _CARRY, E_MIX), f32)]
    windows = [(tuple(d for d in shape if d is not None), a.dtype, n)
               for (shape, n, _), a in zip(operands, args)]
    windows.append(((STEP, D_MODEL), f32, 2))
    return pl.pallas_call(
        functools.partial(body, n_steps=n_steps, n_blocks=batch * n_steps),
        grid=(batch * n_steps + 1,),
        in_specs=[pl.BlockSpec(shape, imap, pipeline_mode=pl.Buffered(1) if n == 1 else None)
                  for shape, n, imap in operands],
        out_specs=pl.BlockSpec((None, STEP, D_MODEL), lambda g: (*out_block(g), 0)),
        out_shape=jax.ShapeDtypeStruct(x.shape, f32),
        scratch_shapes=[pltpu.VMEM(s, d) for s, d in scratch],
        compiler_params=pltpu.CompilerParams(
            dimension_semantics=("arbitrary",), vmem_limit_bytes=_vmem_limit(windows, scratch)),
        name=f"{kind}_layer",
    )(*args)


def kernel(x, mem, w_in, w_mem_kv, w_out, rel_bias, conv_w, ln_g, ln_b):
    batch, seq, d = x.shape
    assert d == D_MODEL and seq % STEP == 0 and STEP >= HIST
    assert mem.shape == (batch, N_MEM, D_MODEL)
    w_in_bf = w_in.astype(bf16)
    w_out_bf = w_out.astype(bf16)
    kvm = _kv_mem_all(mem.reshape(batch * N_MEM, D_MODEL), w_mem_kv.astype(bf16))
    kvm = kvm.reshape(DEPTH, batch, N_MEM, 2 * E_MEM)
    tshift = jnp.pad(rel_bias[:, :, 1:], ((0, 0), (0, 0), (0, BASE_W - (N_REL - 1))))
    for layer in range(DEPTH):
        g = ln_g[layer].reshape(1, D_MODEL)
        b = ln_b[layer].reshape(1, D_MODEL)
        w_l = w_in_bf[layer]
        if layer % 2 == 0:
            w_qvt = jnp.concatenate([w_l[:, _Q0:_Q0 + E_MIX], w_l[:, _V0:_V0 + E_MIX]], axis=1).T
            x = _layer_call("attn", x, [w_l, w_qvt], w_out_bf[layer], kvm[layer],
                            _tile_bias(tshift[layer // 2]), g, b)
        else:
            x = _layer_call("conv", x, [w_l], w_out_bf[layer], kvm[layer],
                            conv_w[layer // 2], g, b)
    return x
```
